```python
import math
import jax
import jax.numpy as jnp
from jax import lax
import numpy as np

D_MODEL = 2048
BATCH = 4
SEQ = 4096
DEPTH = 1
DEC_BATCH = 32
DEC_SEQ = 4
PAST_LEN = 16384
PAGE_SIZE = 128

N_HEADS = 8
QK_DIM = 128
V_DIM = 2 * QK_DIM
Q_WIDTH = N_HEADS * 2 * QK_DIM
K_WIDTH = N_HEADS * 2 * QK_DIM
V_WIDTH = N_HEADS * V_DIM
ATTN_SCALE = QK_DIM ** -0.5
D_POOL = D_MODEL // 2
POOL_WINDOWS = (2, 4, 8, 16)
N_POOL_GROUPS = len(POOL_WINDOWS)
POOL_GROUP_DIM = D_POOL // N_POOL_GROUPS
POOL_HIST = max(POOL_WINDOWS) - 1
IN_WIDTH = Q_WIDTH + K_WIDTH + V_WIDTH + D_POOL + 2 * D_MODEL
SPLIT_POINTS = (Q_WIDTH, Q_WIDTH + K_WIDTH, Q_WIDTH + K_WIDTH + V_WIDTH,
                Q_WIDTH + K_WIDTH + V_WIDTH + D_POOL, Q_WIDTH + K_WIDTH + V_WIDTH + D_POOL + D_MODEL)
Q_BLOCK = 128
N_GROUPS = 4
EXPERTS_PER_GROUP = 8
N_EXPERTS = N_GROUPS * EXPERTS_PER_GROUP
TOP_K = 2
D_EXPERT = D_MODEL // 4
LN_EPS = 1e-5
DEEPNORM_ALPHA = (2.0 * DEPTH) ** 0.25
DEEPNORM_BETA = (8.0 * DEPTH) ** -0.25

kernel_name = "hybrid_pool_diffattn_hiermoe_step"

F32 = jnp.float32


def _layer_norm(x, g, b):
    xf = x.astype(F32)
    mu = jnp.mean(xf, -1, keepdims=True)
    xc = xf - mu
    var = jnp.mean(xc * xc, -1, keepdims=True)
    return (xc * lax.rsqrt(var + LN_EPS) * g.astype(F32) + b.astype(F32)).astype(x.dtype)


def _combined_projection(x, w_in):
    B, T, _ = x.shape
    p = jnp.einsum('btd,de->bte', x, w_in)
    q, k, v, u, ga, gb = jnp.split(p, SPLIT_POINTS, axis=-1)
    q = q.reshape(B, T, N_HEADS, 2, QK_DIM)
    k = k.reshape(B, T, N_HEADS, 2 * QK_DIM)
    v = v.reshape(B, T, N_HEADS, V_DIM)
    return q, k, v, u, ga, gb


def _attend_stats(q, k, v, mask):
    s = jnp.einsum('bqhmd,bkhmd->bhmqk', q, k).astype(F32) * ATTN_SCALE
    if mask is not None:
        s = jnp.where(mask[:, None, None], s, -jnp.inf)
    m = jnp.max(s, -1)
    p = jnp.exp(s - m[..., None])
    l = jnp.sum(p, -1)
    acc = jnp.einsum('bhmqk,bkhd->bhmqd', p, v.astype(F32))
    return m, l, acc


def _merge_stats(a, b):
    m1, l1, c1 = a
    m2, l2, c2 = b
    m = jnp.maximum(m1, m2)
    e1 = jnp.exp(m1 - m)
    e2 = jnp.exp(m2 - m)
    return m, l1 * e1 + l2 * e2, c1 * e1[..., None] + c2 * e2[..., None]


def _diff_finalize(stats, lam, lam_init, subln_g):
    _, l, acc = stats
    o = acc / l[..., None]
    o = o[:, :, 0] - lam * o[:, :, 1]
    o = o * lax.rsqrt(jnp.mean(o * o, -1, keepdims=True) + LN_EPS) * subln_g.astype(F32) * (1.0 - lam_init)
    B, H, Tq, dv = o.shape
    return jnp.transpose(o, (0, 2, 1, 3)).reshape(B, Tq, H * dv)


def _prompt_attention(q, k, v, lam, lam_init, subln_g):
    B, T = q.shape[:2]
    nb = T // Q_BLOCK
    kk = k.reshape(B, T, N_HEADS, 2, QK_DIM)
    qb = jnp.moveaxis(q.reshape(B, nb, Q_BLOCK, N_HEADS, 2, QK_DIM), 1, 0)
    k_pos = jnp.arange(T)

    def block(args):
        q_blk, i = args
        q_pos = i * Q_BLOCK + jnp.arange(Q_BLOCK)
        mask = (k_pos[None, :] <= q_pos[:, None])[None]
        return _diff_finalize(_attend_stats(q_blk, kk, v, mask), lam, lam_init, subln_g)

    o = lax.map(block, (qb, jnp.arange(nb)))
    return jnp.moveaxis(o, 0, 1).reshape(B, T, N_HEADS * V_DIM)


def _sample_attention(q, k_new, v_new, cache_k, cache_v, layer, page_table, lam, lam_init, subln_g):
    B, Tq = q.shape[:2]
    kn = k_new.reshape(B, Tq, N_HEADS, 2, QK_DIM)
    causal = jnp.tril(jnp.ones((Tq, Tq), bool))[None]
    stats0 = _attend_stats(q, kn, v_new, causal)

    def step(stats, pages):
        kp = cache_k[layer, pages].reshape(B, PAGE_SIZE, N_HEADS, 2, QK_DIM)
        vp = cache_v[layer, pages]
        return _merge_stats(stats, _attend_stats(q, kp, vp, None)), None

    stats, _ = lax.scan(step, stats0, page_table.T)
    return _diff_finalize(stats, lam, lam_init, subln_g)


def _pool_mixer(u_prev, u, pos0, w_pool_lin, pool_scale):
    B, T, _ = u.shape
    buf = jnp.concatenate([u_prev, u], axis=1)
    csum = jnp.concatenate([jnp.zeros((B, 1, D_POOL), F32), jnp.cumsum(buf.astype(F32), axis=1)], axis=1)
    pos = pos0 + jnp.arange(T)
    hi = csum[:, POOL_HIST + 1:POOL_HIST + 1 + T]
    means = []
    for g, w in enumerate(POOL_WINDOWS):
        sl = slice(g * POOL_GROUP_DIM, (g + 1) * POOL_GROUP_DIM)
        lo = csum[:, POOL_HIST + 1 - w:POOL_HIST + 1 - w + T, sl]
        cnt = jnp.minimum(w, pos + 1).astype(F32)
        means.append((hi[..., sl] - lo) / cnt[None, :, None])
    pooled = jnp.concatenate(means, -1) - u.astype(F32)
    pooled = pooled.reshape(B, T, N_POOL_GROUPS, POOL_GROUP_DIM)
    mixed = jnp.einsum('btgc,gce->btge', pooled, w_pool_lin.astype(F32)).reshape(B, T, D_POOL)
    return mixed * pool_scale.astype(F32), buf[:, -POOL_HIST:]


def _hier_moe(x, w_rg, b_rg, w_re, b_re, w_gate, w_up, w_down):
    B, T, D = x.shape
    xt = x.reshape(B * T, D)
    g_logits = jnp.matmul(xt, w_rg).astype(F32) + b_rg.astype(F32)
    g_prob = jax.nn.softmax(g_logits, -1)
    g_sel = jnp.argmax(g_logits, -1)
    g_w = jnp.take_along_axis(g_prob, g_sel[:, None], -1)
    e_logits = (jnp.matmul(xt, w_re).astype(F32) + b_re.astype(F32)).reshape(-1, N_GROUPS, EXPERTS_PER_GROUP)
    e_logits = jnp.take_along_axis(e_logits, g_sel[:, None, None], axis=1)[:, 0]
    top_v, top_i = lax.top_k(e_logits, TOP_K)
    e_w = jax.nn.softmax(top_v, -1) * g_w
    e_id = g_sel[:, None] * EXPERTS_PER_GROUP + top_i
    combine = jnp.sum(jax.nn.one_hot(e_id, N_EXPERTS, dtype=F32) * e_w[..., None], axis=1)
    out = jnp.zeros((B * T, D), F32)
    for e in range(N_EXPERTS):
        h = jax.nn.silu(jnp.matmul(xt, w_gate[e])) * jnp.matmul(xt, w_up[e])
        out = out + combine[:, e:e + 1] * jnp.matmul(h, w_down[e]).astype(F32)
    return out.reshape(B, T, D).astype(x.dtype)


def _layer_tail(x, attn_heads, pooled, ga, gb, w_attn_o, w_pool_o, w_out, ln1_g, ln1_b, moe, ln2_g, ln2_b):
    attn_branch = jnp.einsum('bte,ed->btd', attn_heads.astype(x.dtype), w_attn_o)
    pool_branch = jnp.einsum('btc,cd->btd', pooled.astype(x.dtype), w_pool_o)
    merged = jax.nn.sigmoid(ga) * pool_branch + jax.nn.sigmoid(gb) * attn_branch
    mix = jnp.einsum('btd,de->bte', merged, w_out)
    x = _layer_norm(DEEPNORM_ALPHA * x + mix, ln1_g, ln1_b)
    return _layer_norm(DEEPNORM_ALPHA * x + _hier_moe(x, *moe), ln2_g, ln2_b)


def setup_inputs(seed: int = 0) -> dict:
    key = jax.random.key(seed)
    ks = jax.random.split(key, 32)
    n_pages = PAST_LEN // PAGE_SIZE
    n_used = DEC_BATCH * n_pages
    n_pool_pages = n_used + n_used // 4

    def nrm(k, shape, scale):
        return jax.random.normal(k, shape, F32) * scale

    v_lo = Q_WIDTH + K_WIDTH
    w_in = nrm(ks[6], (DEPTH, D_MODEL, IN_WIDTH), D_MODEL ** -0.5)
    w_in = w_in.at[..., v_lo:v_lo + V_WIDTH].multiply(DEEPNORM_BETA)
    return {
        "x_prompt": nrm(ks[0], (BATCH, SEQ, D_MODEL), 1.0),
        "x_sample": nrm(ks[1], (DEC_BATCH, DEC_SEQ, D_MODEL), 1.0),
        "cache_k": nrm(ks[2], (DEPTH, n_pool_pages, PAGE_SIZE, N_HEADS, 2 * QK_DIM), 1.0),
        "cache_v": nrm(ks[3], (DEPTH, n_pool_pages, PAGE_SIZE, N_HEADS, V_DIM), 1.0),
        "state_pool": nrm(ks[4], (DEPTH, DEC_BATCH, POOL_HIST, D_POOL), 1.0),
        "page_table": jax.random.permutation(ks[5], n_pool_pages)[:n_used].reshape(DEC_BATCH, n_pages).astype(jnp.int32),
        "w_in": w_in,
        "lambda_q1": nrm(ks[7], (DEPTH, QK_DIM), 0.1),
        "lambda_k1": nrm(ks[8], (DEPTH, QK_DIM), 0.1),
        "lambda_q2": nrm(ks[9], (DEPTH, QK_DIM), 0.1),
        "lambda_k2": nrm(ks[10], (DEPTH, QK_DIM), 0.1),
        "subln_g": 1.0 + nrm(ks[11], (DEPTH, V_DIM), 0.02),
        "w_pool_lin": nrm(ks[12], (DEPTH, N_POOL_GROUPS, POOL_GROUP_DIM, POOL_GROUP_DIM), POOL_GROUP_DIM ** -0.5),
        "pool_scale": 1.0 + nrm(ks[13], (DEPTH, D_POOL), 0.1),
        "w_pool_o": nrm(ks[14], (DEPTH, D_POOL, D_MODEL), D_POOL ** -0.5 * DEEPNORM_BETA),
        "w_attn_o": nrm(ks[15], (DEPTH, V_WIDTH, D_MODEL), V_WIDTH ** -0.5 * DEEPNORM_BETA),
        "w_out": nrm(ks[16], (DEPTH, D_MODEL, D_MODEL), D_MODEL ** -0.5 * DEEPNORM_BETA),
        "ln1_g": 1.0 + nrm(ks[17], (DEPTH, D_MODEL), 0.02),
        "ln1_b": nrm(ks[18], (DEPTH, D_MODEL), 0.02),
        "w_router_group": nrm(ks[19], (DEPTH, D_MODEL, N_GROUPS), D_MODEL ** -0.5),
        "b_router_group": nrm(ks[20], (DEPTH, N_GROUPS), 0.01),
        "w_router_expert": nrm(ks[21], (DEPTH, D_MODEL, N_EXPERTS), D_MODEL ** -0.5),
        "b_router_expert": nrm(ks[22], (DEPTH, N_EXPERTS), 0.01),
        "w_exp_gate": nrm(ks[23], (DEPTH, N_EXPERTS, D_MODEL, D_EXPERT), D_MODEL ** -0.5),
        "w_exp_up": nrm(ks[24], (DEPTH, N_EXPERTS, D_MODEL, D_EXPERT), D_MODEL ** -0.5),
        "w_exp_down": nrm(ks[25], (DEPTH, N_EXPERTS, D_EXPERT, D_MODEL), D_EXPERT ** -0.5 * DEEPNORM_BETA),
        "ln2_g": 1.0 + nrm(ks[26], (DEPTH, D_MODEL), 0.02),
        "ln2_b": nrm(ks[27], (DEPTH, D_MODEL), 0.02),
    }


def reference(x_prompt, x_sample, cache_k, cache_v, state_pool, page_table,
              w_in, lambda_q1, lambda_k1, lambda_q2, lambda_k2, subln_g,
              w_pool_lin, pool_scale, w_pool_o, w_attn_o, w_out, ln1_g, ln1_b,
              w_router_group, b_router_group, w_router_expert, b_router_expert,
              w_exp_gate, w_exp_up, w_exp_down, ln2_g, ln2_b):
    past_len = page_table.shape[1] * PAGE_SIZE
    hp, hs = x_prompt, x_sample
    kp_rows, vp_rows, pool_p, ks_rows, vs_rows, pool_s = [], [], [], [], [], []
    for layer in range(DEPTH):
        lam_init = 0.8 - 0.6 * math.exp(-0.3 * layer)
        lam = (jnp.exp(jnp.sum(lambda_q1[layer].astype(F32) * lambda_k1[layer].astype(F32)))
               - jnp.exp(jnp.sum(lambda_q2[layer].astype(F32) * lambda_k2[layer].astype(F32))) + lam_init)
        moe = (w_router_group[layer], b_router_group[layer], w_router_expert[layer], b_router_expert[layer],
               w_exp_gate[layer], w_exp_up[layer], w_exp_down[layer])
        tail = (w_attn_o[layer], w_pool_o[layer], w_out[layer], ln1_g[layer], ln1_b[layer], moe,
                ln2_g[layer], ln2_b[layer])

        q, k, v, u, ga, gb = _combined_projection(hp, w_in[layer])
        attn = _prompt_attention(q, k, v, lam, lam_init, subln_g[layer])
        zeros_hist = jnp.zeros((hp.shape[0], POOL_HIST, D_POOL), u.dtype)
        pooled, new_pool = _pool_mixer(zeros_hist, u, 0, w_pool_lin[layer], pool_scale[layer])
        kp_rows.append(k)
        vp_rows.append(v)
        pool_p.append(new_pool)
        hp = _layer_tail(hp, attn, pooled, ga, gb, *tail)

        q, k, v, u, ga, gb = _combined_projection(hs, w_in[layer])
        attn = _sample_attention(q, k, v, cache_k, cache_v, layer, page_table, lam, lam_init, subln_g[layer])
        pooled, new_pool = _pool_mixer(state_pool[layer].astype(u.dtype), u, past_len,
                                       w_pool_lin[layer], pool_scale[layer])
        ks_rows.append(k)
        vs_rows.append(v)
        pool_s.append(new_pool)
        hs = _layer_tail(hs, attn, pooled, ga, gb, *tail)

    return (hp, hs, jnp.stack(kp_rows), jnp.stack(vp_rows), jnp.stack(pool_p),
            jnp.stack(ks_rows), jnp.stack(vs_rows), jnp.stack(pool_s))
```

```python
import functools
import math

import jax
import jax.numpy as jnp
from jax import lax
from jax.experimental import pallas as pl
from jax.experimental.pallas import tpu as pltpu

F32 = jnp.float32
BF16 = jnp.bfloat16

N_HEADS = 8
QK_DIM = 128
V_DIM = 2 * QK_DIM
HEAD_W = 2 * QK_DIM
ATTN_SCALE = QK_DIM ** -0.5
POOL_WINDOWS = (2, 4, 8, 16)
N_POOL_GROUPS = len(POOL_WINDOWS)
POOL_HIST = max(POOL_WINDOWS) - 1
HIST_ROWS = POOL_HIST + 1
N_GROUPS = 4
EXPERTS_PER_GROUP = 8
N_EXPERTS = N_GROUPS * EXPERTS_PER_GROUP
TOP_K = 2
LN_EPS = 1e-5

LANES = 128
SUBLANES = 8
VMEM_LIMIT_BYTES = 56 * 1024 * 1024
ROUTER_PAD = LANES

NEG_INF = float("-inf")


def _cparams(sem):
    return pltpu.CompilerParams(dimension_semantics=sem, vmem_limit_bytes=VMEM_LIMIT_BYTES)


def _row_tile(n, pref):
    t = min(pref, n)
    while n % t:
        t //= 2
    assert t >= SUBLANES, (n, pref)
    return t


def _proj_body(x_ref, w_ref, q_ref, k_ref, kb_ref, v_ref, vb_ref, u_ref, g_ref, xb_ref, *, bounds):
    j = pl.program_id(1)

    @pl.when(j == 0)
    def _cast():
        xb_ref[...] = x_ref[...].astype(BF16)

    acc = jnp.dot(xb_ref[...], w_ref[...], preferred_element_type=F32)
    q_hi, k_hi, v_hi, u_hi = bounds

    @pl.when(j < q_hi)
    def _q():
        q_ref[...] = (acc * ATTN_SCALE).astype(BF16)

    @pl.when((j >= q_hi) & (j < k_hi))
    def _k():
        k_ref[...] = acc
        kb_ref[...] = acc.astype(BF16)

    @pl.when((j >= k_hi) & (j < v_hi))
    def _v():
        v_ref[...] = acc
        vb_ref[...] = acc.astype(BF16)

    @pl.when((j >= v_hi) & (j < u_hi))
    def _u():
        u_ref[...] = acc

    @pl.when(j >= u_hi)
    def _g():
        g_ref[...] = acc


def _in_projection(x, w_bf, d_pool):
    n, d = x.shape
    qw = N_HEADS * HEAD_W
    gw = w_bf.shape[1] - 3 * qw - d_pool
    tn = 512
    tm = _row_tile(n, 512)
    bounds = (qw // tn, 2 * qw // tn, 3 * qw // tn, (3 * qw + d_pool) // tn)
    q_hi, k_hi, v_hi, u_hi = bounds
    n_col = w_bf.shape[1] // tn

    def seg(lo, hi):
        return lambda i, j: (i, jnp.clip(j - lo, 0, hi - lo - 1))

    out_shape = (
        jax.ShapeDtypeStruct((n, qw), BF16),
        jax.ShapeDtypeStruct((n, qw), F32), jax.ShapeDtypeStruct((n, qw), BF16),
        jax.ShapeDtypeStruct((n, qw), F32), jax.ShapeDtypeStruct((n, qw), BF16),
        jax.ShapeDtypeStruct((n, d_pool), F32),
        jax.ShapeDtypeStruct((n, gw), F32),
    )
    blk = lambda lo, hi: pl.BlockSpec((tm, tn), seg(lo, hi))
    return pl.pallas_call(
        functools.partial(_proj_body, bounds=bounds),
        grid=(n // tm, n_col),
        in_specs=[pl.BlockSpec((tm, d), lambda i, j: (i, 0)),
                  pl.BlockSpec((d, tn), lambda i, j: (0, j))],
        out_specs=(blk(0, q_hi), blk(q_hi, k_hi), blk(q_hi, k_hi), blk(k_hi, v_hi), blk(k_hi, v_hi),
                   blk(v_hi, u_hi), blk(u_hi, n_col)),
        out_shape=out_shape,
        scratch_shapes=[pltpu.VMEM((tm, d), BF16)],
        compiler_params=_cparams(("arbitrary", "arbitrary")),
        name="in_projection",
    )(x, w_bf)


def _diff_finalize(o1, o2, lam, gain, lam_init):
    o = o1 - lam * o2
    o = o * lax.rsqrt(jnp.mean(o * o, axis=-1, keepdims=True) + LN_EPS)
    return o * gain * (1.0 - lam_init)


def _flash_body(lam_ref, q_ref, k_ref, v_ref, g_ref, o_ref, acc_ref, m_ref, l_ref, *, tq, lam_init):
    i = pl.program_id(2)
    m_ref[...] = jnp.full(m_ref.shape, NEG_INF, F32)
    l_ref[...] = jnp.zeros(l_ref.shape, F32)
    acc_ref[...] = jnp.zeros(acc_ref.shape, F32)
    q = q_ref[...]

    def block(j, masked):
        start = pl.multiple_of(j * tq, tq)
        ks = k_ref[pl.ds(start, tq), :]
        vs = v_ref[pl.ds(start, tq), :]
        for mp in range(2):
            sl = slice(mp * QK_DIM, (mp + 1) * QK_DIM)
            s = lax.dot_general(q[:, sl], ks[:, sl], (((1,), (1,)), ((), ())), preferred_element_type=F32)
            if masked:
                row = lax.broadcasted_iota(jnp.int32, s.shape, 0)
                col = lax.broadcasted_iota(jnp.int32, s.shape, 1)
                s = jnp.where(col <= row, s, NEG_INF)
            m_old = m_ref[mp]
            m_new = jnp.maximum(m_old, jnp.max(s, axis=-1, keepdims=True))
            p = jnp.exp(s - m_new)
            a = jnp.exp(m_old - m_new)
            l_ref[mp] = a * l_ref[mp] + jnp.sum(p, axis=-1, keepdims=True)
            acc_ref[mp] = a * acc_ref[mp] + jnp.dot(p.astype(BF16), vs, preferred_element_type=F32)
            m_ref[mp] = m_new

    def off_diag(j, c):
        block(j, False)
        return c

    lax.fori_loop(0, i, off_diag, 0)
    block(i, True)
    o1 = acc_ref[0] / l_ref[0]
    o2 = acc_ref[1] / l_ref[1]
    o_ref[...] = _diff_finalize(o1, o2, lam_ref[0], g_ref[...], lam_init).astype(o_ref.dtype)


def _prompt_attention(q_bf, k_bf, v_bf, lam, gain, batch, seq, lam_init):
    tq = _row_tile(seq, 512)
    nq = seq // tq
    kv_spec = pl.BlockSpec((seq, HEAD_W), lambda b, h, i: (b, h))
    return pl.pallas_call(
        functools.partial(_flash_body, tq=tq, lam_init=lam_init),
        grid=(batch, N_HEADS, nq),
        in_specs=[pl.BlockSpec(memory_space=pltpu.SMEM),
                  pl.BlockSpec((tq, HEAD_W), lambda b, h, i: (b * nq + i, h)),
                  kv_spec, kv_spec,
                  pl.BlockSpec((1, V_DIM), lambda b, h, i: (0, 0))],
        out_specs=pl.BlockSpec((tq, V_DIM), lambda b, h, i: (b * nq + i, h)),
        out_shape=jax.ShapeDtypeStruct((batch * seq, N_HEADS * V_DIM), BF16),
        scratch_shapes=[pltpu.VMEM((2, tq, V_DIM), F32), pltpu.VMEM((2, tq, 1), F32), pltpu.VMEM((2, tq, 1), F32)],
        compiler_params=_cparams(("arbitrary", "arbitrary", "arbitrary")),
        name="prompt_attention",
    )(lam, q_bf, k_bf, v_bf, gain)


def _decode_body(pt_ref, lam_ref, q_ref, kn_ref, vn_ref, kc_ref, vc_ref, bn_ref, bp_ref, g_ref, o_ref,
                 acc_ref, m_ref, l_ref, *, lam_init):
    del pt_ref
    p_idx = pl.program_id(1)
    half = q_ref.shape[0] // 2
    q = q_ref[...]

    def attend(k_bf, v_bf, bias):
        s = lax.dot_general(q, k_bf, (((1,), (1,)), ((), ())), preferred_element_type=F32) + bias
        m_old = m_ref[...]
        m_new = jnp.maximum(m_old, jnp.max(s, axis=-1, keepdims=True))
        p = jnp.exp(s - m_new)
        a = jnp.exp(m_old - m_new)
        l_ref[...] = a * l_ref[...] + jnp.sum(p, axis=-1, keepdims=True)
        acc_ref[...] = a * acc_ref[...] + jnp.dot(p.astype(BF16), v_bf, preferred_element_type=F32)
        m_ref[...] = m_new

    @pl.when(p_idx == 0)
    def _new_tokens():
        m_ref[...] = jnp.full(m_ref.shape, NEG_INF, F32)
        l_ref[...] = jnp.zeros(l_ref.shape, F32)
        acc_ref[...] = jnp.zeros(acc_ref.shape, F32)
        attend(kn_ref[...], vn_ref[...], bn_ref[...])

    attend(kc_ref[...].astype(BF16), vc_ref[...].astype(BF16), bp_ref[...])

    @pl.when(p_idx == pl.num_programs(1) - 1)
    def _finish():
        o = acc_ref[...] / l_ref[...]
        o_ref[...] = _diff_finalize(o[:half], o[half:], lam_ref[0], g_ref[...], lam_init).astype(o_ref.dtype)


def _decode_bias(rows, cols, t_new, causal):
    r = jnp.arange(rows, dtype=jnp.int32)[:, None]
    c = jnp.arange(cols, dtype=jnp.int32)[None, :]
    ok = (c % N_HEADS) == ((r // t_new) % N_HEADS)
    if causal:
        ok = ok & ((c // N_HEADS) <= (r % t_new))
    return jnp.where(ok, 0.0, NEG_INF).astype(F32)


def _sample_attention(q_rows, k_new, v_new, cache_k, cache_v, page_table, lam, gain, t_new, lam_init):
    nb, rows, _ = q_rows.shape
    n_pages = page_table.shape[1]
    page_rows = cache_k.shape[1]
    new_rows = k_new.shape[1]
    bias_new = _decode_bias(rows, new_rows, t_new, True)
    bias_page = _decode_bias(rows, page_rows, t_new, False)
    cache_spec = pl.BlockSpec((None, page_rows, HEAD_W), lambda b, p, pt: (pt[b * n_pages + p], 0, 0))
    per_b = lambda r: pl.BlockSpec((None, r, HEAD_W), lambda b, p, pt: (b, 0, 0))
    fixed = lambda shape: pl.BlockSpec(shape, lambda b, p, pt: (0, 0))
    grid_spec = pltpu.PrefetchScalarGridSpec(
        num_scalar_prefetch=1,
        grid=(nb, n_pages),
        in_specs=[pl.BlockSpec(memory_space=pltpu.SMEM),
                  per_b(rows), per_b(new_rows), per_b(new_rows),
                  cache_spec, cache_spec,
                  fixed((rows, new_rows)), fixed((rows, page_rows)), fixed((1, V_DIM))],
        out_specs=pl.BlockSpec((None, rows // 2, V_DIM), lambda b, p, pt: (b, 0, 0)),
        scratch_shapes=[pltpu.VMEM((rows, V_DIM), F32), pltpu.VMEM((rows, 1), F32), pltpu.VMEM((rows, 1), F32)],
    )
    return pl.pallas_call(
        functools.partial(_decode_body, lam_init=lam_init),
        grid_spec=grid_spec,
        out_shape=jax.ShapeDtypeStruct((nb, rows // 2, V_DIM), F32),
        compiler_params=_cparams(("arbitrary", "arbitrary")),
        name="sample_attention",
    )(page_table.reshape(-1), lam, q_rows, k_new, v_new, cache_k, cache_v, bias_new, bias_page, gain)


def _pool_body(hist_ref, prev_ref, u_ref, w_ref, sc_ref, o_ref, *, tm, pos0):
    i = pl.program_id(1)
    u = u_ref[...]
    head = jnp.where(i == 0, hist_ref[...], prev_ref[...])
    ext = jnp.concatenate([head, u], axis=0)
    pos = pos0 + i * tm + lax.broadcasted_iota(jnp.int32, (tm, 1), 0)
    gdim = w_ref.shape[1]
    for g, win in enumerate(POOL_WINDOWS):
        sl = slice(g * gdim, (g + 1) * gdim)
        s = ext[:, sl]
        span = 1
        while span < win:
            s = s + pltpu.roll(s, span, axis=0)
            span *= 2
        cnt = jnp.minimum(win, pos + 1).astype(F32)
        pooled = s[HIST_ROWS:] / cnt - u[:, sl]
        mixed = jnp.dot(pooled.astype(BF16), w_ref[g], preferred_element_type=F32)
        o_ref[:, sl] = (mixed * sc_ref[:, sl]).astype(o_ref.dtype)


def _pool_mixer(hist, u, w_lin_bf, scale, batch, seq, pos0):
    d_pool = u.shape[1]
    tm = _row_tile(seq, 512)
    nt = seq // tm
    per = tm // HIST_ROWS
    return pl.pallas_call(
        functools.partial(_pool_body, tm=tm, pos0=pos0),
        grid=(batch, nt),
        in_specs=[pl.BlockSpec((None, HIST_ROWS, d_pool), lambda b, i: (b, 0, 0)),
                  pl.BlockSpec((HIST_ROWS, d_pool), lambda b, i: (jnp.maximum((b * nt + i) * per - 1, 0), 0)),
                  pl.BlockSpec((tm, d_pool), lambda b, i: (b * nt + i, 0)),
                  pl.BlockSpec(w_lin_bf.shape, lambda b, i: (0, 0, 0)),
                  pl.BlockSpec((1, d_pool), lambda b, i: (0, 0))],
        out_specs=pl.BlockSpec((tm, d_pool), lambda b, i: (b * nt + i, 0)),
        out_shape=jax.ShapeDtypeStruct(u.shape, BF16),
        compiler_params=_cparams(("arbitrary", "arbitrary")),
        name="pool_mixer",
    )(hist, u, u, w_lin_bf, scale)


def _merge_body(pool_ref, attn_ref, ga_ref, gb_ref, wp_ref, wa_ref, o_ref):
    pool_branch = jnp.dot(pool_ref[...], wp_ref[...], preferred_element_type=F32)
    attn_branch = jnp.dot(attn_ref[...], wa_ref[...], preferred_element_type=F32)
    merged = jax.nn.sigmoid(ga_ref[...]) * pool_branch + jax.nn.sigmoid(gb_ref[...]) * attn_branch
    o_ref[...] = merged.astype(o_ref.dtype)


def _gated_merge(pooled_bf, attn_bf, gates, w_pool_o_bf, w_attn_o_bf):
    n, d_pool = pooled_bf.shape
    d = w_pool_o_bf.shape[1]
    tm = _row_tile(n, 512)
    tn = 512
    nj = d // tn
    return pl.pallas_call(
        _merge_body,
        grid=(n // tm, nj),
        in_specs=[pl.BlockSpec((tm, d_pool), lambda i, j: (i, 0)),
                  pl.BlockSpec((tm, attn_bf.shape[1]), lambda i, j: (i, 0)),
                  pl.BlockSpec((tm, tn), lambda i, j: (i, j)),
                  pl.BlockSpec((tm, tn), lambda i, j: (i, nj + j)),
                  pl.BlockSpec((d_pool, tn), lambda i, j: (0, j)),
                  pl.BlockSpec((attn_bf.shape[1], tn), lambda i, j: (0, j))],
        out_specs=pl.BlockSpec((tm, tn), lambda i, j: (i, j)),
        out_shape=jax.ShapeDtypeStruct((n, d), BF16),
        compiler_params=_cparams(("arbitrary", "arbitrary")),
        name="gated_merge",
    )(pooled_bf, attn_bf, gates, gates, w_pool_o_bf, w_attn_o_bf)


def _layer_norm(y, g, b):
    mu = jnp.mean(y, axis=-1, keepdims=True)
    yc = y - mu
    var = jnp.mean(yc * yc, axis=-1, keepdims=True)
    return yc * lax.rsqrt(var + LN_EPS) * g + b


def _mix_ln_body(m_ref, w_ref, x_ref, g_ref, b_ref, rh_ref, rl_ref, x1_ref, x1b_ref, lg_ref, *, alpha):
    mix = jnp.dot(m_ref[...], w_ref[...], preferred_element_type=F32)
    x1 = _layer_norm(alpha * x_ref[...] + mix, g_ref[...], b_ref[...])
    x1_ref[...] = x1
    hi = x1.astype(BF16)
    x1b_ref[...] = hi
    lo = (x1 - hi.astype(F32)).astype(BF16)
    lg_ref[...] = (jnp.dot(hi, rh_ref[...], preferred_element_type=F32)
                   + jnp.dot(lo, rh_ref[...], preferred_element_type=F32)
                   + jnp.dot(hi, rl_ref[...], preferred_element_type=F32))


def _mix_ln_router(merged_bf, w_out_bf, x, ln_g, ln_b, r_hi, r_lo, alpha):
    n, d = x.shape
    tm = _row_tile(n, 256)
    row = lambda i: (i, 0)
    fixed = lambda i: (0, 0)
    return pl.pallas_call(
        functools.partial(_mix_ln_body, alpha=alpha),
        grid=(n // tm,),
        in_specs=[pl.BlockSpec((tm, d), row), pl.BlockSpec((d, d), fixed), pl.BlockSpec((tm, d), row),
                  pl.BlockSpec((1, d), fixed), pl.BlockSpec((1, d), fixed),
                  pl.BlockSpec((d, ROUTER_PAD), fixed), pl.BlockSpec((d, ROUTER_PAD), fixed)],
        out_specs=(pl.BlockSpec((tm, d), row), pl.BlockSpec((tm, d), row), pl.BlockSpec((tm, ROUTER_PAD), row)),
        out_shape=(jax.ShapeDtypeStruct((n, d), F32), jax.ShapeDtypeStruct((n, d), BF16),
                   jax.ShapeDtypeStruct((n, ROUTER_PAD), F32)),
        compiler_params=_cparams(("arbitrary",)),
        name="mix_ln_router",
    )(merged_bf, w_out_bf, x, ln_g, ln_b, r_hi, r_lo)


def _moe_body(te_ref, nt_ref, x_ref, rw_ref, wg_ref, wu_ref, wd_ref, o_ref):
    del te_ref
    t = pl.program_id(0)

    @pl.when(t < nt_ref[0])
    def _run():
        x = x_ref[...]
        gate = jnp.dot(x, wg_ref[...].astype(BF16), preferred_element_type=F32)
        up = jnp.dot(x, wu_ref[...].astype(BF16), preferred_element_type=F32)
        h = (gate * jax.nn.sigmoid(gate)) * up
        y = jnp.dot(h.astype(BF16), wd_ref[...].astype(BF16), preferred_element_type=F32)
        o_ref[...] = y * rw_ref[...]

    @pl.when(t >= nt_ref[0])
    def _skip():
        o_ref[...] = jnp.zeros(o_ref.shape, o_ref.dtype)


def _moe_experts(tile_expert, n_tiles_used, x_sorted_bf, row_w, w_gate, w_up, w_down, tm):
    r, d = x_sorted_bf.shape
    de = w_gate.shape[2]
    n_tiles = r // tm
    grid_spec = pltpu.PrefetchScalarGridSpec(
        num_scalar_prefetch=2,
        grid=(n_tiles,),
        in_specs=[pl.BlockSpec((tm, d), lambda t, te, nt: (t, 0)),
                  pl.BlockSpec((tm, 1), lambda t, te, nt: (t, 0)),
                  pl.BlockSpec((None, d, de), lambda t, te, nt: (te[t], 0, 0)),
                  pl.BlockSpec((None, d, de), lambda t, te, nt: (te[t], 0, 0)),
                  pl.BlockSpec((None, de, d), lambda t, te, nt: (te[t], 0, 0))],
        out_specs=pl.BlockSpec((tm, d), lambda t, te, nt: (t, 0)),
    )
    return pl.pallas_call(
        _moe_body,
        grid_spec=grid_spec,
        out_shape=jax.ShapeDtypeStruct((r, d), F32),
        compiler_params=_cparams(("arbitrary",)),
        name="moe_experts",
    )(tile_expert, n_tiles_used, x_sorted_bf, row_w, w_gate, w_up, w_down)


def _route(logits, b_rg, b_re):
    g_logits = logits[:, :N_GROUPS] + b_rg
    g_prob = jax.nn.softmax(g_logits, -1)
    g_sel = jnp.argmax(g_logits, -1)
    g_w = jnp.take_along_axis(g_prob, g_sel[:, None], -1)
    e_logits = (logits[:, N_GROUPS:N_GROUPS + N_EXPERTS] + b_re).reshape(-1, N_GROUPS, EXPERTS_PER_GROUP)
    e_logits = jnp.take_along_axis(e_logits, g_sel[:, None, None], axis=1)[:, 0]
    top_v, top_i = lax.top_k(e_logits, TOP_K)
    e_w = jax.nn.softmax(top_v, -1) * g_w
    e_id = g_sel[:, None] * EXPERTS_PER_GROUP + top_i
    return e_id.astype(jnp.int32), e_w


def _dispatch_plan(e_id, e_w, tm):
    n = e_id.shape[0]
    flat_e = e_id.reshape(-1)
    n_assign = flat_e.shape[0]
    onehot = (flat_e[:, None] == jnp.arange(N_EXPERTS, dtype=jnp.int32)[None, :]).astype(jnp.int32)
    ranks = jnp.cumsum(onehot, axis=0) - onehot
    rank = jnp.sum(ranks * onehot, axis=1)
    counts = jnp.sum(onehot, axis=0)
    padded = ((counts + tm - 1) // tm) * tm
    ends = jnp.cumsum(padded)
    starts = ends - padded
    pos = starts[flat_e] + rank
    n_tiles = (n_assign + tm - 1) // tm + N_EXPERTS
    rows = n_tiles * tm
    row_token = jnp.zeros((rows,), jnp.int32).at[pos].set(jnp.arange(n_assign, dtype=jnp.int32) // TOP_K)
    row_w = jnp.zeros((rows,), F32).at[pos].set(e_w.reshape(-1))
    tile_start = jnp.arange(n_tiles, dtype=jnp.int32) * tm
    tile_expert = jnp.sum((tile_start[:, None] >= ends[None, :]).astype(jnp.int32), axis=1)
    n_used = (ends[-1] // tm).astype(jnp.int32)
    last_expert = jnp.sum((ends[-1] - 1 >= ends).astype(jnp.int32))
    tile_expert = jnp.minimum(tile_expert, last_expert).astype(jnp.int32)
    return row_token, row_w.reshape(rows, 1), tile_expert, n_used.reshape(1), pos.reshape(n, TOP_K)


def _ln2_body(x_ref, y_ref, g_ref, b_ref, o_ref, *, alpha):
    o_ref[...] = _layer_norm(alpha * x_ref[...] + y_ref[...], g_ref[...], b_ref[...])


def _residual_ln(x1, moe_out, ln_g, ln_b, alpha):
    n, d = x1.shape
    tm = _row_tile(n, 512)
    row = lambda i: (i, 0)
    fixed = lambda i: (0, 0)
    return pl.pallas_call(
        functools.partial(_ln2_body, alpha=alpha),
        grid=(n // tm,),
        in_specs=[pl.BlockSpec((tm, d), row), pl.BlockSpec((tm, d), row),
                  pl.BlockSpec((1, d), fixed), pl.BlockSpec((1, d), fixed)],
        out_specs=pl.BlockSpec((tm, d), row),
        out_shape=jax.ShapeDtypeStruct((n, d), F32),
        compiler_params=_cparams(("arbitrary",)),
        name="residual_ln",
    )(x1, moe_out, ln_g, ln_b)


def kernel(x_prompt, x_sample, cache_k, cache_v, state_pool, page_table, w_in, lambda_q1, lambda_k1, lambda_q2, lambda_k2, subln_g, w_pool_lin, pool_scale, w_pool_o, w_attn_o, w_out, ln1_g, ln1_b, w_router_group, b_router_group, w_router_expert, b_router_expert, w_exp_gate, w_exp_up, w_exp_down, ln2_g, ln2_b):
    depth = w_in.shape[0]
    assert depth == 1, "one layer per step"
    layer = 0
    batch, seq, d_model = x_prompt.shape
    dec_batch, dec_seq, _ = x_sample.shape
    n_pages = page_table.shape[1]
    page_size = cache_k.shape[2]
    d_pool = w_pool_lin.shape[1] * w_pool_lin.shape[2]
    past_len = n_pages * page_size
    alpha = (2.0 * depth) ** 0.25
    lam_init = 0.8 - 0.6 * math.exp(-0.3 * layer)
    n_p, n_s = batch * seq, dec_batch * dec_seq

    lam = (jnp.exp(jnp.sum(lambda_q1[layer].astype(F32) * lambda_k1[layer].astype(F32)))
           - jnp.exp(jnp.sum(lambda_q2[layer].astype(F32) * lambda_k2[layer].astype(F32))) + lam_init).reshape(1)
    gain = subln_g[layer].astype(F32).reshape(1, V_DIM)

    w_in_bf = w_in[layer].astype(BF16)
    w_lin_bf = w_pool_lin[layer].astype(BF16)
    scale = pool_scale[layer].astype(F32).reshape(1, d_pool)
    w_pool_o_bf = w_pool_o[layer].astype(BF16)
    w_attn_o_bf = w_attn_o[layer].astype(BF16)
    w_out_bf = w_out[layer].astype(BF16)
    ln1g, ln1b = ln1_g[layer].reshape(1, d_model), ln1_b[layer].reshape(1, d_model)
    ln2g, ln2b = ln2_g[layer].reshape(1, d_model), ln2_b[layer].reshape(1, d_model)
    w_router = jnp.concatenate([w_router_group[layer], w_router_expert[layer]], axis=1).astype(F32)
    w_router = jnp.pad(w_router, ((0, 0), (0, ROUTER_PAD - w_router.shape[1])))
    r_hi = w_router.astype(BF16)
    r_lo = (w_router - r_hi.astype(F32)).astype(BF16)

    xp = x_prompt.reshape(n_p, d_model)
    q_p, k_p, kb_p, v_p, vb_p, u_p, g_p = _in_projection(xp, w_in_bf, d_pool)
    attn_p = _prompt_attention(q_p, kb_p, vb_p, lam, gain, batch, seq, lam_init)
    pooled_p = _pool_mixer(jnp.zeros((batch, HIST_ROWS, d_pool), F32), u_p, w_lin_bf, scale, batch, seq, 0)
    merged_p = _gated_merge(pooled_p, attn_p, g_p, w_pool_o_bf, w_attn_o_bf)
    x1_p, x1b_p, lg_p = _mix_ln_router(merged_p, w_out_bf, xp, ln1g, ln1b, r_hi, r_lo, alpha)

    xs = x_sample.reshape(n_s, d_model)
    q_s, k_s, kb_s, v_s, vb_s, u_s, g_s = _in_projection(xs, w_in_bf, d_pool)
    q5 = q_s.reshape(dec_batch, dec_seq, N_HEADS, 2, QK_DIM).transpose(0, 3, 2, 1, 4)
    zq = jnp.zeros_like(q5[:, 0])
    q_rows = jnp.stack([jnp.concatenate([q5[:, 0], zq], -1), jnp.concatenate([zq, q5[:, 1]], -1)], axis=1)
    q_rows = q_rows.reshape(dec_batch, 2 * N_HEADS * dec_seq, HEAD_W)
    new_rows = LANES
    assert dec_seq * N_HEADS <= new_rows
    pad_new = lambda a: jnp.pad(a.reshape(dec_batch, dec_seq * N_HEADS, HEAD_W),
                                ((0, 0), (0, new_rows - dec_seq * N_HEADS), (0, 0)))
    ck = cache_k[layer].reshape(cache_k.shape[1], page_size * N_HEADS, HEAD_W)
    cv = cache_v[layer].reshape(cache_v.shape[1], page_size * N_HEADS, V_DIM)
    attn_s = _sample_attention(q_rows, pad_new(kb_s), pad_new(vb_s), ck, cv, page_table.astype(jnp.int32),
                               lam, gain, dec_seq, lam_init)
    attn_s = attn_s.reshape(dec_batch, N_HEADS, dec_seq, V_DIM).transpose(0, 2, 1, 3).reshape(n_s, N_HEADS * V_DIM)
    seq_pad = -(-dec_seq // SUBLANES) * SUBLANES
    hist_s = jnp.pad(state_pool[layer].astype(F32), ((0, 0), (HIST_ROWS - POOL_HIST, 0), (0, 0)))
    u_s3 = u_s.reshape(dec_batch, dec_seq, d_pool)
    u_pad = jnp.pad(u_s3, ((0, 0), (0, seq_pad - dec_seq), (0, 0))).reshape(dec_batch * seq_pad, d_pool)
    pooled_s = _pool_mixer(hist_s, u_pad, w_lin_bf, scale, dec_batch, seq_pad, past_len)
    pooled_s = pooled_s.reshape(dec_batch, seq_pad, d_pool)[:, :dec_seq].reshape(n_s, d_pool)
    merged_s = _gated_merge(pooled_s, attn_s.astype(BF16), g_s, w_pool_o_bf, w_attn_o_bf)
    x1_s, x1b_s, lg_s = _mix_ln_router(merged_s, w_out_bf, xs, ln1g, ln1b, r_hi, r_lo, alpha)

    tm_moe = 256
    logits = jnp.concatenate([lg_p, lg_s], axis=0)
    e_id, e_w = _route(logits, b_router_group[layer].astype(F32), b_router_expert[layer].astype(F32))
    row_token, row_w, tile_expert, n_used, pos = _dispatch_plan(e_id, e_w, tm_moe)
    x1b = jnp.concatenate([x1b_p, x1b_s], axis=0)
    x_sorted = jnp.take(x1b, row_token, axis=0)
    y_sorted = _moe_experts(tile_expert, n_used, x_sorted, row_w, w_exp_gate[layer], w_exp_up[layer],
                            w_exp_down[layer], tm_moe)
    moe = jnp.take(y_sorted, pos[:, 0], axis=0) + jnp.take(y_sorted, pos[:, 1], axis=0)
    y_p = _residual_ln(x1_p, moe[:n_p], ln2g, ln2b, alpha)
    y_s = _residual_ln(x1_s, moe[n_p:], ln2g, ln2b, alpha)

    assert seq >= POOL_HIST
    new_pool_p = u_p.reshape(batch, seq, d_pool)[:, seq - POOL_HIST:]
    new_pool_s = jnp.concatenate([state_pool[layer].astype(F32), u_s3], axis=1)[:, -POOL_HIST:]
    return (y_p.reshape(batch, seq, d_model), y_s.reshape(dec_batch, dec_seq, d_model),
            k_p.reshape(1, batch, seq, N_HEADS, HEAD_W), v_p.reshape(1, batch, seq, N_HEADS, V_DIM),
            new_pool_p[None],
            k_s.reshape(1, dec_batch, dec_seq, N_HEADS, HEAD_W), v_s.reshape(1, dec_batch, dec_seq, N_HEADS, V_DIM),
            new_pool_s[None])
```

```python
import functools
import math

import jax
import jax.numpy as jnp
from jax import lax
from jax.experimental import pallas as pl
from jax.experimental.pallas import tpu as pltpu

F32 = jnp.float32
BF16 = jnp.bfloat16

N_HEADS = 8
QK_DIM = 128
V_DIM = 2 * QK_DIM
HEAD_W = 2 * QK_DIM
ATTN_SCALE = QK_DIM ** -0.5
LOG2_E = math.log2(math.e)
POOL_WINDOWS = (2, 4, 8, 16)
N_POOL_GROUPS = len(POOL_WINDOWS)
POOL_HIST = max(POOL_WINDOWS) - 1
HIST_ROWS = POOL_HIST + 1
N_GROUPS = 4
EXPERTS_PER_GROUP = 8
N_EXPERTS = N_GROUPS * EXPERTS_PER_GROUP
TOP_K = 2
LN_EPS = 1e-5

LANES = 128
SUBLANES = 8
VMEM_LIMIT_BYTES = 56 * 1024 * 1024
ROUTER_PAD = LANES
DECODE_PAGES_PER_STEP = (4, 2, 1)

NEG_INF = float("-inf")


def _cparams(sem):
    return pltpu.CompilerParams(dimension_semantics=sem, vmem_limit_bytes=VMEM_LIMIT_BYTES)


def _row_tile(n, pref):
    t = min(pref, n)
    while n % t:
        t //= 2
    assert t >= SUBLANES, (n, pref)
    return t


def _proj_body(x_ref, w_ref, q_ref, k_ref, kb_ref, v_ref, vb_ref, u_ref, g_ref, xb_ref, *, bounds):
    j = pl.program_id(1)

    @pl.when(j == 0)
    def _cast():
        xb_ref[...] = x_ref[...].astype(BF16)

    acc = jnp.dot(xb_ref[...], w_ref[...], preferred_element_type=F32)
    q_hi, k_hi, v_hi, u_hi = bounds

    @pl.when(j < q_hi)
    def _q():
        q_ref[...] = (acc * (ATTN_SCALE * LOG2_E)).astype(BF16)

    @pl.when((j >= q_hi) & (j < k_hi))
    def _k():
        k_ref[...] = acc
        kb_ref[...] = acc.astype(BF16)

    @pl.when((j >= k_hi) & (j < v_hi))
    def _v():
        v_ref[...] = acc
        vb_ref[...] = acc.astype(BF16)

    @pl.when((j >= v_hi) & (j < u_hi))
    def _u():
        u_ref[...] = acc

    @pl.when(j >= u_hi)
    def _g():
        g_ref[...] = acc


def _in_projection(x, w_bf, d_pool):
    n, d = x.shape
    qw = N_HEADS * HEAD_W
    gw = w_bf.shape[1] - 3 * qw - d_pool
    tn = 1024
    tm = _row_tile(n, 512)
    bounds = (qw // tn, 2 * qw // tn, 3 * qw // tn, (3 * qw + d_pool) // tn)
    q_hi, k_hi, v_hi, u_hi = bounds
    n_col = w_bf.shape[1] // tn

    def seg(lo, hi):
        return lambda i, j: (i, jnp.clip(j - lo, 0, hi - lo - 1))

    out_shape = (
        jax.ShapeDtypeStruct((n, qw), BF16),
        jax.ShapeDtypeStruct((n, qw), F32), jax.ShapeDtypeStruct((n, qw), BF16),
        jax.ShapeDtypeStruct((n, qw), F32), jax.ShapeDtypeStruct((n, qw), BF16),
        jax.ShapeDtypeStruct((n, d_pool), F32),
        jax.ShapeDtypeStruct((n, gw), F32),
    )
    blk = lambda lo, hi: pl.BlockSpec((tm, tn), seg(lo, hi))
    return pl.pallas_call(
        functools.partial(_proj_body, bounds=bounds),
        grid=(n // tm, n_col),
        in_specs=[pl.BlockSpec((tm, d), lambda i, j: (i, 0)),
                  pl.BlockSpec((d, tn), lambda i, j: (0, j))],
        out_specs=(blk(0, q_hi), blk(q_hi, k_hi), blk(q_hi, k_hi), blk(k_hi, v_hi), blk(k_hi, v_hi),
                   blk(v_hi, u_hi), blk(u_hi, n_col)),
        out_shape=out_shape,
        scratch_shapes=[pltpu.VMEM((tm, d), BF16)],
        compiler_params=_cparams(("arbitrary", "arbitrary")),
        name="in_projection",
    )(x, w_bf)


def _diff_finalize(o1, o2, lam, gain, lam_init):
    o = o1 - lam * o2
    o = o * lax.rsqrt(jnp.mean(o * o, axis=-1, keepdims=True) + LN_EPS)
    return o * gain * (1.0 - lam_init)


def _flash_body(lam_ref, q_ref, k_ref, v_ref, g_ref, o_ref, qbd_ref, acc_ref, m_ref, l_ref, *, tq, lam_init):
    i = pl.program_id(2)
    q = q_ref[...]
    zeros = jnp.zeros((tq, QK_DIM), q.dtype)
    qbd_ref[:tq, :] = jnp.concatenate([q[:, :QK_DIM], zeros], axis=1)
    qbd_ref[tq:, :] = jnp.concatenate([zeros, q[:, QK_DIM:]], axis=1)
    m_ref[...] = jnp.full(m_ref.shape, NEG_INF, F32)
    l_ref[...] = jnp.zeros(l_ref.shape, F32)
    acc_ref[...] = jnp.zeros(acc_ref.shape, F32)

    def block(j, masked):
        start = pl.multiple_of(j * tq, tq)
        ks = k_ref[pl.ds(start, tq), :]
        vs = v_ref[pl.ds(start, tq), :]
        s = lax.dot_general(ks, qbd_ref[...], (((1,), (1,)), ((), ())), preferred_element_type=F32)
        if masked:
            key = lax.broadcasted_iota(jnp.int32, (tq, tq), 0)
            qry = lax.broadcasted_iota(jnp.int32, (tq, tq), 1)
            visible = key <= qry
            s = jnp.concatenate([jnp.where(visible, s[:, :tq], NEG_INF), jnp.where(visible, s[:, tq:], NEG_INF)],
                                axis=1)
        m_old = m_ref[...]
        m_new = jnp.maximum(m_old, jnp.max(s, axis=0, keepdims=True))
        p = jnp.exp2(s - m_new)
        a = jnp.exp2(m_old - m_new)
        l_ref[...] = a * l_ref[...] + jnp.sum(p, axis=0, keepdims=True)
        pv = lax.dot_general(vs, p.astype(BF16), (((0,), (0,)), ((), ())), preferred_element_type=F32)
        acc_ref[...] = a * acc_ref[...] + pv
        m_ref[...] = m_new

    def off_diag(j, c):
        block(j, False)
        return c

    lax.fori_loop(0, i, off_diag, 0)
    block(i, True)
    o_t = acc_ref[...] / l_ref[...]
    o = o_t[:, :tq] - lam_ref[0] * o_t[:, tq:]
    o = o * lax.rsqrt(jnp.mean(o * o, axis=0, keepdims=True) + LN_EPS) * g_ref[...] * (1.0 - lam_init)
    o_ref[...] = o.T.astype(o_ref.dtype)


def _prompt_attention(q_bf, k_bf, v_bf, lam, gain_col, batch, seq, lam_init):
    tq = _row_tile(seq, 512)
    nq = seq // tq
    kv_spec = pl.BlockSpec((seq, HEAD_W), lambda b, h, i: (b, h))
    return pl.pallas_call(
        functools.partial(_flash_body, tq=tq, lam_init=lam_init),
        grid=(batch, N_HEADS, nq),
        in_specs=[pl.BlockSpec(memory_space=pltpu.SMEM),
                  pl.BlockSpec((tq, HEAD_W), lambda b, h, i: (b * nq + i, h)),
                  kv_spec, kv_spec,
                  pl.BlockSpec((V_DIM, 1), lambda b, h, i: (0, 0))],
        out_specs=pl.BlockSpec((tq, V_DIM), lambda b, h, i: (b * nq + i, h)),
        out_shape=jax.ShapeDtypeStruct((batch * seq, N_HEADS * V_DIM), BF16),
        scratch_shapes=[pltpu.VMEM((2 * tq, HEAD_W), BF16), pltpu.VMEM((V_DIM, 2 * tq), F32),
                        pltpu.VMEM((1, 2 * tq), F32), pltpu.VMEM((1, 2 * tq), F32)],
        compiler_params=_cparams(("arbitrary", "arbitrary", "arbitrary")),
        name="prompt_attention",
    )(lam, q_bf, k_bf, v_bf, gain_col)


def _decode_body(pt_ref, lam_ref, q_ref, kn_ref, vn_ref, *rest, n_chunk, lam_init):
    del pt_ref
    kc_refs, vc_refs = rest[:n_chunk], rest[n_chunk:2 * n_chunk]
    bn_ref, bp_ref, g_ref, o_ref, acc_ref, m_ref, l_ref = rest[2 * n_chunk:]
    p_idx = pl.program_id(1)
    half = q_ref.shape[0] // 2
    q = q_ref[...]

    def attend(blocks):
        scores = [lax.dot_general(q, k_bf, (((1,), (1,)), ((), ())), preferred_element_type=F32) + bias
                  for k_bf, _, bias in blocks]
        m_old = m_ref[...]
        m_new = jnp.maximum(m_old, jnp.max(functools.reduce(jnp.maximum, scores), axis=-1, keepdims=True))
        probs = [jnp.exp2(s - m_new) for s in scores]
        a = jnp.exp2(m_old - m_new)
        l_ref[...] = a * l_ref[...] + jnp.sum(functools.reduce(jnp.add, probs), axis=-1, keepdims=True)
        pv = [jnp.dot(p.astype(BF16), v_bf, preferred_element_type=F32) for p, (_, v_bf, _) in zip(probs, blocks)]
        acc_ref[...] = a * acc_ref[...] + functools.reduce(jnp.add, pv)
        m_ref[...] = m_new

    @pl.when(p_idx == 0)
    def _new_tokens():
        m_ref[...] = jnp.full(m_ref.shape, NEG_INF, F32)
        l_ref[...] = jnp.zeros(l_ref.shape, F32)
        acc_ref[...] = jnp.zeros(acc_ref.shape, F32)
        attend([(kn_ref[...], vn_ref[...], bn_ref[...])])

    attend([(kc[...].astype(BF16), vc[...].astype(BF16), bp_ref[...]) for kc, vc in zip(kc_refs, vc_refs)])

    @pl.when(p_idx == pl.num_programs(1) - 1)
    def _finish():
        o = acc_ref[...] / l_ref[...]
        o_ref[...] = _diff_finalize(o[:half], o[half:], lam_ref[0], g_ref[...], lam_init).astype(o_ref.dtype)


def _decode_bias(rows, cols, t_new, causal):
    r = jnp.arange(rows, dtype=jnp.int32)[:, None]
    c = jnp.arange(cols, dtype=jnp.int32)[None, :]
    ok = (c % N_HEADS) == ((r // t_new) % N_HEADS)
    if causal:
        ok = ok & ((c // N_HEADS) <= (r % t_new))
    return jnp.where(ok, 0.0, NEG_INF).astype(F32)


def _sample_attention(q_rows, k_new, v_new, cache_k, cache_v, page_table, lam, gain, t_new, lam_init):
    nb, rows, _ = q_rows.shape
    n_pages = page_table.shape[1]
    page_rows = cache_k.shape[1]
    new_rows = k_new.shape[1]
    bias_new = _decode_bias(rows, new_rows, t_new, True)
    bias_page = _decode_bias(rows, page_rows, t_new, False)
    n_chunk = max(c for c in DECODE_PAGES_PER_STEP if n_pages % c == 0)

    def cache_spec(r):
        return pl.BlockSpec((None, page_rows, HEAD_W), lambda b, p, pt: (pt[b * n_pages + p * n_chunk + r], 0, 0))

    per_b = lambda r: pl.BlockSpec((None, r, HEAD_W), lambda b, p, pt: (b, 0, 0))
    fixed = lambda shape: pl.BlockSpec(shape, lambda b, p, pt: (0, 0))
    cache_specs = [cache_spec(r) for r in range(n_chunk)]
    grid_spec = pltpu.PrefetchScalarGridSpec(
        num_scalar_prefetch=1,
        grid=(nb, n_pages // n_chunk),
        in_specs=[pl.BlockSpec(memory_space=pltpu.SMEM),
                  per_b(rows), per_b(new_rows), per_b(new_rows),
                  *cache_specs, *cache_specs,
                  fixed((rows, new_rows)), fixed((rows, page_rows)), fixed((1, V_DIM))],
        out_specs=pl.BlockSpec((None, rows // 2, V_DIM), lambda b, p, pt: (b, 0, 0)),
        scratch_shapes=[pltpu.VMEM((rows, V_DIM), F32), pltpu.VMEM((rows, 1), F32), pltpu.VMEM((rows, 1), F32)],
    )
    return pl.pallas_call(
        functools.partial(_decode_body, n_chunk=n_chunk, lam_init=lam_init),
        grid_spec=grid_spec,
        out_shape=jax.ShapeDtypeStruct((nb, rows // 2, V_DIM), F32),
        compiler_params=_cparams(("arbitrary", "arbitrary")),
        name="sample_attention",
    )(page_table.reshape(-1), lam, q_rows, k_new, v_new, *([cache_k] * n_chunk), *([cache_v] * n_chunk),
      bias_new, bias_page, gain)


def _pool_body(hist_ref, prev_ref, u_ref, w_ref, sc_ref, o_ref, *, tm, pos0):
    i = pl.program_id(1)
    u = u_ref[...]
    head = jnp.where(i == 0, hist_ref[...], prev_ref[...])
    ext = jnp.concatenate([head, u], axis=0)
    pos = pos0 + i * tm + lax.broadcasted_iota(jnp.int32, (tm, 1), 0)
    gdim = w_ref.shape[1]
    for g, win in enumerate(POOL_WINDOWS):
        sl = slice(g * gdim, (g + 1) * gdim)
        s = ext[:, sl]
        span = 1
        while span < win:
            s = s + pltpu.roll(s, span, axis=0)
            span *= 2
        cnt = jnp.minimum(win, pos + 1).astype(F32)
        pooled = s[HIST_ROWS:] / cnt - u[:, sl]
        mixed = jnp.dot(pooled.astype(BF16), w_ref[g], preferred_element_type=F32)
        o_ref[:, sl] = (mixed * sc_ref[:, sl]).astype(o_ref.dtype)


def _pool_mixer(hist, u, w_lin_bf, scale, batch, seq, pos0):
    d_pool = u.shape[1]
    tm = _row_tile(seq, 512)
    nt = seq // tm
    per = tm // HIST_ROWS
    return pl.pallas_call(
        functools.partial(_pool_body, tm=tm, pos0=pos0),
        grid=(batch, nt),
        in_specs=[pl.BlockSpec((None, HIST_ROWS, d_pool), lambda b, i: (b, 0, 0)),
                  pl.BlockSpec((HIST_ROWS, d_pool), lambda b, i: (jnp.maximum((b * nt + i) * per - 1, 0), 0)),
                  pl.BlockSpec((tm, d_pool), lambda b, i: (b * nt + i, 0)),
                  pl.BlockSpec(w_lin_bf.shape, lambda b, i: (0, 0, 0)),
                  pl.BlockSpec((1, d_pool), lambda b, i: (0, 0))],
        out_specs=pl.BlockSpec((tm, d_pool), lambda b, i: (b * nt + i, 0)),
        out_shape=jax.ShapeDtypeStruct(u.shape, BF16),
        compiler_params=_cparams(("arbitrary", "arbitrary")),
        name="pool_mixer",
    )(hist, u, u, w_lin_bf, scale)


def _merge_body(pool_ref, attn_ref, ga_ref, gb_ref, wp_ref, wa_ref, o_ref):
    pool_branch = jnp.dot(pool_ref[...], wp_ref[...], preferred_element_type=F32)
    attn_branch = jnp.dot(attn_ref[...], wa_ref[...], preferred_element_type=F32)
    merged = jax.nn.sigmoid(ga_ref[...]) * pool_branch + jax.nn.sigmoid(gb_ref[...]) * attn_branch
    o_ref[...] = merged.astype(o_ref.dtype)


def _gated_merge(pooled_bf, attn_bf, gates, w_pool_o_bf, w_attn_o_bf):
    n, d_pool = pooled_bf.shape
    d = w_pool_o_bf.shape[1]
    tm = _row_tile(n, 512)
    tn = 512
    nj = d // tn
    return pl.pallas_call(
        _merge_body,
        grid=(n // tm, nj),
        in_specs=[pl.BlockSpec((tm, d_pool), lambda i, j: (i, 0)),
                  pl.BlockSpec((tm, attn_bf.shape[1]), lambda i, j: (i, 0)),
                  pl.BlockSpec((tm, tn), lambda i, j: (i, j)),
                  pl.BlockSpec((tm, tn), lambda i, j: (i, nj + j)),
                  pl.BlockSpec((d_pool, tn), lambda i, j: (0, j)),
                  pl.BlockSpec((attn_bf.shape[1], tn), lambda i, j: (0, j))],
        out_specs=pl.BlockSpec((tm, tn), lambda i, j: (i, j)),
        out_shape=jax.ShapeDtypeStruct((n, d), BF16),
        compiler_params=_cparams(("arbitrary", "arbitrary")),
        name="gated_merge",
    )(pooled_bf, attn_bf, gates, gates, w_pool_o_bf, w_attn_o_bf)


def _layer_norm(y, g, b):
    mu = jnp.mean(y, axis=-1, keepdims=True)
    yc = y - mu
    var = jnp.mean(yc * yc, axis=-1, keepdims=True)
    return yc * lax.rsqrt(var + LN_EPS) * g + b


def _mix_ln_body(m_ref, w_ref, x_ref, g_ref, b_ref, rh_ref, rl_ref, x1_ref, x1b_ref, lg_ref, *, alpha):
    mix = jnp.dot(m_ref[...], w_ref[...], preferred_element_type=F32)
    x1 = _layer_norm(alpha * x_ref[...] + mix, g_ref[...], b_ref[...])
    x1_ref[...] = x1
    hi = x1.astype(BF16)
    x1b_ref[...] = hi
    lo = (x1 - hi.astype(F32)).astype(BF16)
    lg_ref[...] = (jnp.dot(hi, rh_ref[...], preferred_element_type=F32)
                   + jnp.dot(lo, rh_ref[...], preferred_element_type=F32)
                   + jnp.dot(hi, rl_ref[...], preferred_element_type=F32))


def _mix_ln_router(merged_bf, w_out_bf, x, ln_g, ln_b, r_hi, r_lo, alpha):
    n, d = x.shape
    tm = _row_tile(n, 256)
    row = lambda i: (i, 0)
    fixed = lambda i: (0, 0)
    return pl.pallas_call(
        functools.partial(_mix_ln_body, alpha=alpha),
        grid=(n // tm,),
        in_specs=[pl.BlockSpec((tm, d), row), pl.BlockSpec((d, d), fixed), pl.BlockSpec((tm, d), row),
                  pl.BlockSpec((1, d), fixed), pl.BlockSpec((1, d), fixed),
                  pl.BlockSpec((d, ROUTER_PAD), fixed), pl.BlockSpec((d, ROUTER_PAD), fixed)],
        out_specs=(pl.BlockSpec((tm, d), row), pl.BlockSpec((tm, d), row), pl.BlockSpec((tm, ROUTER_PAD), row)),
        out_shape=(jax.ShapeDtypeStruct((n, d), F32), jax.ShapeDtypeStruct((n, d), BF16),
                   jax.ShapeDtypeStruct((n, ROUTER_PAD), F32)),
        compiler_params=_cparams(("arbitrary",)),
        name="mix_ln_router",
    )(merged_bf, w_out_bf, x, ln_g, ln_b, r_hi, r_lo)


def _moe_body(te_ref, nt_ref, x_ref, rw_ref, wg_ref, wu_ref, wd_ref, o_ref):
    del te_ref
    t = pl.program_id(0)

    @pl.when(t < nt_ref[0])
    def _run():
        x = x_ref[...]
        gate = jnp.dot(x, wg_ref[...].astype(BF16), preferred_element_type=F32)
        up = jnp.dot(x, wu_ref[...].astype(BF16), preferred_element_type=F32)
        h = (gate * jax.nn.sigmoid(gate)) * up
        y = jnp.dot(h.astype(BF16), wd_ref[...].astype(BF16), preferred_element_type=F32)
        o_ref[...] = y * rw_ref[...]

    @pl.when(t >= nt_ref[0])
    def _skip():
        o_ref[...] = jnp.zeros(o_ref.shape, o_ref.dtype)


def _moe_experts(tile_expert, n_tiles_used, x_sorted_bf, row_w, w_gate, w_up, w_down, tm):
    r, d = x_sorted_bf.shape
    de = w_gate.shape[2]
    n_tiles = r // tm
    grid_spec = pltpu.PrefetchScalarGridSpec(
        num_scalar_prefetch=2,
        grid=(n_tiles,),
        in_specs=[pl.BlockSpec((tm, d), lambda t, te, nt: (t, 0)),
                  pl.BlockSpec((tm, 1), lambda t, te, nt: (t, 0)),
                  pl.BlockSpec((None, d, de), lambda t, te, nt: (te[t], 0, 0)),
                  pl.BlockSpec((None, d, de), lambda t, te, nt: (te[t], 0, 0)),
                  pl.BlockSpec((None, de, d), lambda t, te, nt: (te[t], 0, 0))],
        out_specs=pl.BlockSpec((tm, d), lambda t, te, nt: (t, 0)),
    )
    return pl.pallas_call(
        _moe_body,
        grid_spec=grid_spec,
        out_shape=jax.ShapeDtypeStruct((r, d), F32),
        compiler_params=_cparams(("arbitrary",)),
        name="moe_experts",
    )(tile_expert, n_tiles_used, x_sorted_bf, row_w, w_gate, w_up, w_down)


def _route(logits, b_rg, b_re):
    g_logits = logits[:, :N_GROUPS] + b_rg
    g_prob = jax.nn.softmax(g_logits, -1)
    g_sel = jnp.argmax(g_logits, -1)
    g_w = jnp.take_along_axis(g_prob, g_sel[:, None], -1)
    e_logits = (logits[:, N_GROUPS:N_GROUPS + N_EXPERTS] + b_re).reshape(-1, N_GROUPS, EXPERTS_PER_GROUP)
    e_logits = jnp.take_along_axis(e_logits, g_sel[:, None, None], axis=1)[:, 0]
    top_v, top_i = lax.top_k(e_logits, TOP_K)
    e_w = jax.nn.softmax(top_v, -1) * g_w
    e_id = g_sel[:, None] * EXPERTS_PER_GROUP + top_i
    return e_id.astype(jnp.int32), e_w


def _dispatch_plan(e_id, e_w, tm):
    n = e_id.shape[0]
    flat_e = e_id.reshape(-1)
    n_assign = flat_e.shape[0]
    onehot = (flat_e[:, None] == jnp.arange(N_EXPERTS, dtype=jnp.int32)[None, :]).astype(jnp.int32)
    ranks = jnp.cumsum(onehot, axis=0) - onehot
    rank = jnp.sum(ranks * onehot, axis=1)
    counts = jnp.sum(onehot, axis=0)
    padded = ((counts + tm - 1) // tm) * tm
    ends = jnp.cumsum(padded)
    starts = ends - padded
    pos = starts[flat_e] + rank
    n_tiles = (n_assign + tm - 1) // tm + N_EXPERTS
    rows = n_tiles * tm
    row_assign = jnp.zeros((rows,), jnp.int32).at[pos].set(jnp.arange(1, n_assign + 1, dtype=jnp.int32))
    live = row_assign > 0
    row_assign = jnp.maximum(row_assign - 1, 0)
    row_token = row_assign // TOP_K
    row_w = jnp.where(live, e_w.reshape(-1)[row_assign], 0.0)
    tile_start = jnp.arange(n_tiles, dtype=jnp.int32) * tm
    tile_expert = jnp.sum((tile_start[:, None] >= ends[None, :]).astype(jnp.int32), axis=1)
    n_used = (ends[-1] // tm).astype(jnp.int32)
    last_expert = jnp.sum((ends[-1] - 1 >= ends).astype(jnp.int32))
    tile_expert = jnp.minimum(tile_expert, last_expert).astype(jnp.int32)
    return row_token, row_w.reshape(rows, 1), tile_expert, n_used.reshape(1), pos.reshape(n, TOP_K)


def _ln2_body(x_ref, y0_ref, y1_ref, g_ref, b_ref, o_ref, *, alpha):
    moe = y0_ref[...] + y1_ref[...]
    o_ref[...] = _layer_norm(alpha * x_ref[...] + moe, g_ref[...], b_ref[...])


def _residual_ln(x1, y0, y1, ln_g, ln_b, alpha):
    n, d = x1.shape
    tm = _row_tile(n, 512)
    row = lambda i: (i, 0)
    fixed = lambda i: (0, 0)
    return pl.pallas_call(
        functools.partial(_ln2_body, alpha=alpha),
        grid=(n // tm,),
        in_specs=[pl.BlockSpec((tm, d), row), pl.BlockSpec((tm, d), row), pl.BlockSpec((tm, d), row),
                  pl.BlockSpec((1, d), fixed), pl.BlockSpec((1, d), fixed)],
        out_specs=pl.BlockSpec((tm, d), row),
        out_shape=jax.ShapeDtypeStruct((n, d), F32),
        compiler_params=_cparams(("arbitrary",)),
        name="residual_ln",
    )(x1, y0, y1, ln_g, ln_b)


def kernel(x_prompt, x_sample, cache_k, cache_v, state_pool, page_table, w_in, lambda_q1, lambda_k1, lambda_q2, lambda_k2, subln_g, w_pool_lin, pool_scale, w_pool_o, w_attn_o, w_out, ln1_g, ln1_b, w_router_group, b_router_group, w_router_expert, b_router_expert, w_exp_gate, w_exp_up, w_exp_down, ln2_g, ln2_b):
    depth = w_in.shape[0]
    assert depth == 1, "one layer per step"
    layer = 0
    batch, seq, d_model = x_prompt.shape
    dec_batch, dec_seq, _ = x_sample.shape
    n_pages = page_table.shape[1]
    page_size = cache_k.shape[2]
    d_pool = w_pool_lin.shape[1] * w_pool_lin.shape[2]
    past_len = n_pages * page_size
    alpha = (2.0 * depth) ** 0.25
    lam_init = 0.8 - 0.6 * math.exp(-0.3 * layer)
    n_p, n_s = batch * seq, dec_batch * dec_seq

    lam = (jnp.exp(jnp.sum(lambda_q1[layer].astype(F32) * lambda_k1[layer].astype(F32)))
           - jnp.exp(jnp.sum(lambda_q2[layer].astype(F32) * lambda_k2[layer].astype(F32))) + lam_init).reshape(1)
    gain = subln_g[layer].astype(F32).reshape(1, V_DIM)

    w_in_bf = w_in[layer].astype(BF16)
    w_lin_bf = w_pool_lin[layer].astype(BF16)
    scale = pool_scale[layer].astype(F32).reshape(1, d_pool)
    w_pool_o_bf = w_pool_o[layer].astype(BF16)
    w_attn_o_bf = w_attn_o[layer].astype(BF16)
    w_out_bf = w_out[layer].astype(BF16)
    ln1g, ln1b = ln1_g[layer].reshape(1, d_model), ln1_b[layer].reshape(1, d_model)
    ln2g, ln2b = ln2_g[layer].reshape(1, d_model), ln2_b[layer].reshape(1, d_model)
    w_router = jnp.concatenate([w_router_group[layer], w_router_expert[layer]], axis=1).astype(F32)
    w_router = jnp.pad(w_router, ((0, 0), (0, ROUTER_PAD - w_router.shape[1])))
    r_hi = w_router.astype(BF16)
    r_lo = (w_router - r_hi.astype(F32)).astype(BF16)

    xp = x_prompt.reshape(n_p, d_model)
    q_p, k_p, kb_p, v_p, vb_p, u_p, g_p = _in_projection(xp, w_in_bf, d_pool)
    attn_p = _prompt_attention(q_p, kb_p, vb_p, lam, gain.reshape(V_DIM, 1), batch, seq, lam_init)
    pooled_p = _pool_mixer(jnp.zeros((batch, HIST_ROWS, d_pool), F32), u_p, w_lin_bf, scale, batch, seq, 0)
    merged_p = _gated_merge(pooled_p, attn_p, g_p, w_pool_o_bf, w_attn_o_bf)
    x1_p, x1b_p, lg_p = _mix_ln_router(merged_p, w_out_bf, xp, ln1g, ln1b, r_hi, r_lo, alpha)

    xs = x_sample.reshape(n_s, d_model)
    q_s, k_s, kb_s, v_s, vb_s, u_s, g_s = _in_projection(xs, w_in_bf, d_pool)
    q5 = q_s.reshape(dec_batch, dec_seq, N_HEADS, 2, QK_DIM).transpose(0, 3, 2, 1, 4)
    zq = jnp.zeros_like(q5[:, 0])
    q_rows = jnp.stack([jnp.concatenate([q5[:, 0], zq], -1), jnp.concatenate([zq, q5[:, 1]], -1)], axis=1)
    q_rows = q_rows.reshape(dec_batch, 2 * N_HEADS * dec_seq, HEAD_W)
    new_rows = LANES
    assert dec_seq * N_HEADS <= new_rows
    pad_new = lambda a: jnp.pad(a.reshape(dec_batch, dec_seq * N_HEADS, HEAD_W),
                                ((0, 0), (0, new_rows - dec_seq * N_HEADS), (0, 0)))
    ck = cache_k[layer].reshape(cache_k.shape[1], page_size * N_HEADS, HEAD_W)
    cv = cache_v[layer].reshape(cache_v.shape[1], page_size * N_HEADS, V_DIM)
    attn_s = _sample_attention(q_rows, pad_new(kb_s), pad_new(vb_s), ck, cv, page_table.astype(jnp.int32),
                               lam, gain, dec_seq, lam_init)
    attn_s = attn_s.reshape(dec_batch, N_HEADS, dec_seq, V_DIM).transpose(0, 2, 1, 3).reshape(n_s, N_HEADS * V_DIM)
    seq_pad = -(-dec_seq // SUBLANES) * SUBLANES
    hist_s = jnp.pad(state_pool[layer].astype(F32), ((0, 0), (HIST_ROWS - POOL_HIST, 0), (0, 0)))
    u_s3 = u_s.reshape(dec_batch, dec_seq, d_pool)
    u_pad = jnp.pad(u_s3, ((0, 0), (0, seq_pad - dec_seq), (0, 0))).reshape(dec_batch * seq_pad, d_pool)
    pooled_s = _pool_mixer(hist_s, u_pad, w_lin_bf, scale, dec_batch, seq_pad, past_len)
    pooled_s = pooled_s.reshape(dec_batch, seq_pad, d_pool)[:, :dec_seq].reshape(n_s, d_pool)
    merged_s = _gated_merge(pooled_s, attn_s.astype(BF16), g_s, w_pool_o_bf, w_attn_o_bf)
    x1_s, x1b_s, lg_s = _mix_ln_router(merged_s, w_out_bf, xs, ln1g, ln1b, r_hi, r_lo, alpha)

    tm_moe = 256
    logits = jnp.concatenate([lg_p, lg_s], axis=0)
    e_id, e_w = _route(logits, b_router_group[layer].astype(F32), b_router_expert[layer].astype(F32))
    row_token, row_w, tile_expert, n_used, pos = _dispatch_plan(e_id, e_w, tm_moe)
    x1b = jnp.concatenate([x1b_p, x1b_s], axis=0)
    x_sorted = jnp.take(x1b, row_token, axis=0)
    y_sorted = _moe_experts(tile_expert, n_used, x_sorted, row_w, w_exp_gate[layer], w_exp_up[layer],
                            w_exp_down[layer], tm_moe)
    expert_out = lambda lo, hi, k: jnp.take(y_sorted, pos[lo:hi, k], axis=0)
    y_p = _residual_ln(x1_p, expert_out(0, n_p, 0), expert_out(0, n_p, 1), ln2g, ln2b, alpha)
    y_s = _residual_ln(x1_s, expert_out(n_p, n_p + n_s, 0), expert_out(n_p, n_p + n_s, 1), ln2g, ln2b, alpha)

    assert seq >= POOL_HIST
    new_pool_p = u_p.reshape(batch, seq, d_pool)[:, seq - POOL_HIST:]
    new_pool_s = jnp.concatenate([state_pool[layer].astype(F32), u_s3], axis=1)[:, -POOL_HIST:]
    return (y_p.reshape(batch, seq, d_model), y_s.reshape(dec_batch, dec_seq, d_model),
            k_p.reshape(1, batch, seq, N_HEADS, HEAD_W), v_p.reshape(1, batch, seq, N_HEADS, V_DIM),
            new_pool_p[None],
            k_s.reshape(1, dec_batch, dec_seq, N_HEADS, HEAD_W), v_s.reshape(1, dec_batch, dec_seq, N_HEADS, V_DIM),
            new_pool_s[None])
```

```python
import functools
import math

import jax
import jax.numpy as jnp
from jax import lax
from jax.experimental import pallas as pl
from jax.experimental.pallas import tpu as pltpu

F32 = jnp.float32
BF16 = jnp.bfloat16

N_HEADS = 8
QK_DIM = 128
V_DIM = 2 * QK_DIM
HEAD_W = 2 * QK_DIM
ATTN_SCALE = QK_DIM ** -0.5
LOG2_E = math.log2(math.e)
POOL_WINDOWS = (2, 4, 8, 16)
N_POOL_GROUPS = len(POOL_WINDOWS)
POOL_HIST = max(POOL_WINDOWS) - 1
HIST_ROWS = POOL_HIST + 1
N_GROUPS = 4
EXPERTS_PER_GROUP = 8
N_EXPERTS = N_GROUPS * EXPERTS_PER_GROUP
TOP_K = 2
LN_EPS = 1e-5

LANES = 128
SUBLANES = 8
VMEM_LIMIT_BYTES = 56 * 1024 * 1024
ROUTER_PAD = LANES
DECODE_PAGES_PER_STEP = (4, 2, 1)
DECODE_RING = 3

NEG_INF = float("-inf")


def _cparams(sem):
    return pltpu.CompilerParams(dimension_semantics=sem, vmem_limit_bytes=VMEM_LIMIT_BYTES)


def _row_tile(n, pref):
    t = min(pref, n)
    while n % t:
        t //= 2
    assert t >= SUBLANES, (n, pref)
    return t


def _proj_body(x_ref, w_ref, q_ref, k_ref, kb_ref, v_ref, vb_ref, u_ref, g_ref, xb_ref, *, bounds):
    j = pl.program_id(1)

    @pl.when(j == 0)
    def _cast():
        xb_ref[...] = x_ref[...].astype(BF16)

    acc = jnp.dot(xb_ref[...], w_ref[...], preferred_element_type=F32)
    q_hi, k_hi, v_hi, u_hi = bounds

    @pl.when(j < q_hi)
    def _q():
        q_ref[...] = (acc * (ATTN_SCALE * LOG2_E)).astype(BF16)

    @pl.when((j >= q_hi) & (j < k_hi))
    def _k():
        k_ref[...] = acc
        kb_ref[...] = acc.astype(BF16)

    @pl.when((j >= k_hi) & (j < v_hi))
    def _v():
        v_ref[...] = acc
        vb_ref[...] = acc.astype(BF16)

    @pl.when((j >= v_hi) & (j < u_hi))
    def _u():
        u_ref[...] = acc

    @pl.when(j >= u_hi)
    def _g():
        g_ref[...] = acc


def _in_projection(x, w_bf, d_pool):
    n, d = x.shape
    qw = N_HEADS * HEAD_W
    gw = w_bf.shape[1] - 3 * qw - d_pool
    tn = 1024
    tm = _row_tile(n, 512)
    bounds = (qw // tn, 2 * qw // tn, 3 * qw // tn, (3 * qw + d_pool) // tn)
    q_hi, k_hi, v_hi, u_hi = bounds
    n_col = w_bf.shape[1] // tn

    def seg(lo, hi):
        return lambda i, j: (i, jnp.clip(j - lo, 0, hi - lo - 1))

    out_shape = (
        jax.ShapeDtypeStruct((n, qw), BF16),
        jax.ShapeDtypeStruct((n, qw), F32), jax.ShapeDtypeStruct((n, qw), BF16),
        jax.ShapeDtypeStruct((n, qw), F32), jax.ShapeDtypeStruct((n, qw), BF16),
        jax.ShapeDtypeStruct((n, d_pool), F32),
        jax.ShapeDtypeStruct((n, gw), F32),
    )
    blk = lambda lo, hi: pl.BlockSpec((tm, tn), seg(lo, hi))
    return pl.pallas_call(
        functools.partial(_proj_body, bounds=bounds),
        grid=(n // tm, n_col),
        in_specs=[pl.BlockSpec((tm, d), lambda i, j: (i, 0)),
                  pl.BlockSpec((d, tn), lambda i, j: (0, j))],
        out_specs=(blk(0, q_hi), blk(q_hi, k_hi), blk(q_hi, k_hi), blk(k_hi, v_hi), blk(k_hi, v_hi),
                   blk(v_hi, u_hi), blk(u_hi, n_col)),
        out_shape=out_shape,
        scratch_shapes=[pltpu.VMEM((tm, d), BF16)],
        compiler_params=_cparams(("arbitrary", "arbitrary")),
        name="in_projection",
    )(x, w_bf)


def _diff_finalize(o1, o2, lam, gain, lam_init):
    o = o1 - lam * o2
    o = o * lax.rsqrt(jnp.mean(o * o, axis=-1, keepdims=True) + LN_EPS)
    return o * gain * (1.0 - lam_init)


def _flash_body(lam_ref, q_ref, k_ref, v_ref, g_ref, o_ref, qbd_ref, acc_ref, m_ref, l_ref, *, tq, lam_init):
    i = pl.program_id(2)
    q = q_ref[...]
    zeros = jnp.zeros((tq, QK_DIM), q.dtype)
    qbd_ref[:tq, :] = jnp.concatenate([q[:, :QK_DIM], zeros], axis=1)
    qbd_ref[tq:, :] = jnp.concatenate([zeros, q[:, QK_DIM:]], axis=1)
    m_ref[...] = jnp.full(m_ref.shape, NEG_INF, F32)
    l_ref[...] = jnp.zeros(l_ref.shape, F32)
    acc_ref[...] = jnp.zeros(acc_ref.shape, F32)

    def block(j, masked):
        start = pl.multiple_of(j * tq, tq)
        ks = k_ref[pl.ds(start, tq), :]
        vs = v_ref[pl.ds(start, tq), :]
        s = lax.dot_general(ks, qbd_ref[...], (((1,), (1,)), ((), ())), preferred_element_type=F32)
        if masked:
            key = lax.broadcasted_iota(jnp.int32, (tq, tq), 0)
            qry = lax.broadcasted_iota(jnp.int32, (tq, tq), 1)
            visible = key <= qry
            s = jnp.concatenate([jnp.where(visible, s[:, :tq], NEG_INF), jnp.where(visible, s[:, tq:], NEG_INF)],
                                axis=1)
        m_old = m_ref[...]
        m_new = jnp.maximum(m_old, jnp.max(s, axis=0, keepdims=True))
        p = jnp.exp2(s - m_new)
        a = jnp.exp2(m_old - m_new)
        l_ref[...] = a * l_ref[...] + jnp.sum(p, axis=0, keepdims=True)
        pv = lax.dot_general(vs, p.astype(BF16), (((0,), (0,)), ((), ())), preferred_element_type=F32)
        acc_ref[...] = a * acc_ref[...] + pv
        m_ref[...] = m_new

    def off_diag(j, c):
        block(j, False)
        return c

    lax.fori_loop(0, i, off_diag, 0)
    block(i, True)
    o_t = acc_ref[...] / l_ref[...]
    o = o_t[:, :tq] - lam_ref[0] * o_t[:, tq:]
    o = o * lax.rsqrt(jnp.mean(o * o, axis=0, keepdims=True) + LN_EPS) * g_ref[...] * (1.0 - lam_init)
    o_ref[...] = o.T.astype(o_ref.dtype)


def _prompt_attention(q_bf, k_bf, v_bf, lam, gain_col, batch, seq, lam_init):
    tq = _row_tile(seq, 512)
    nq = seq // tq
    kv_spec = pl.BlockSpec((seq, HEAD_W), lambda b, h, i: (b, h))
    return pl.pallas_call(
        functools.partial(_flash_body, tq=tq, lam_init=lam_init),
        grid=(batch, N_HEADS, nq),
        in_specs=[pl.BlockSpec(memory_space=pltpu.SMEM),
                  pl.BlockSpec((tq, HEAD_W), lambda b, h, i: (b * nq + i, h)),
                  kv_spec, kv_spec,
                  pl.BlockSpec((V_DIM, 1), lambda b, h, i: (0, 0))],
        out_specs=pl.BlockSpec((tq, V_DIM), lambda b, h, i: (b * nq + i, h)),
        out_shape=jax.ShapeDtypeStruct((batch * seq, N_HEADS * V_DIM), BF16),
        scratch_shapes=[pltpu.VMEM((2 * tq, HEAD_W), BF16), pltpu.VMEM((V_DIM, 2 * tq), F32),
                        pltpu.VMEM((1, 2 * tq), F32), pltpu.VMEM((1, 2 * tq), F32)],
        compiler_params=_cparams(("arbitrary", "arbitrary", "arbitrary")),
        name="prompt_attention",
    )(lam, q_bf, k_bf, v_bf, gain_col)


def _decode_body(pt_ref, lam_ref, q_ref, kn_ref, vn_ref, kc_hbm, vc_hbm, bn_ref, bp_ref, g_ref, o_ref,
                 kbuf, vbuf, sems, acc_ref, m_ref, l_ref, *, n_chunk, n_groups, lam_init):
    p_idx = pl.program_id(1)
    g = pl.program_id(0) * pl.num_programs(1) + p_idx
    half = q_ref.shape[0] // 2
    q = q_ref[...]

    def page_copies(group, slot):
        copies = []
        for r in range(n_chunk):
            page = pt_ref[group * n_chunk + r]
            copies.append(pltpu.make_async_copy(kc_hbm.at[page], kbuf.at[slot, r], sems.at[slot]))
            copies.append(pltpu.make_async_copy(vc_hbm.at[page], vbuf.at[slot, r], sems.at[slot]))
        return copies

    def start_group(group):
        for cp in page_copies(group, group % DECODE_RING):
            cp.start()

    @pl.when(g == 0)
    def _prime():
        for ahead in range(min(DECODE_RING - 1, n_groups)):
            start_group(ahead)

    @pl.when(g + (DECODE_RING - 1) < n_groups)
    def _prefetch():
        start_group(g + (DECODE_RING - 1))

    slot = g % DECODE_RING
    for cp in page_copies(g, slot):
        cp.wait()
    kc_refs = [kbuf.at[slot, r] for r in range(n_chunk)]
    vc_refs = [vbuf.at[slot, r] for r in range(n_chunk)]

    def attend(blocks):
        scores = [lax.dot_general(q, k_bf, (((1,), (1,)), ((), ())), preferred_element_type=F32) + bias
                  for k_bf, _, bias in blocks]
        m_old = m_ref[...]
        m_new = jnp.maximum(m_old, jnp.max(functools.reduce(jnp.maximum, scores), axis=-1, keepdims=True))
        probs = [jnp.exp2(s - m_new) for s in scores]
        a = jnp.exp2(m_old - m_new)
        l_ref[...] = a * l_ref[...] + jnp.sum(functools.reduce(jnp.add, probs), axis=-1, keepdims=True)
        pv = [jnp.dot(p.astype(BF16), v_bf, preferred_element_type=F32) for p, (_, v_bf, _) in zip(probs, blocks)]
        acc_ref[...] = a * acc_ref[...] + functools.reduce(jnp.add, pv)
        m_ref[...] = m_new

    @pl.when(p_idx == 0)
    def _new_tokens():
        m_ref[...] = jnp.full(m_ref.shape, NEG_INF, F32)
        l_ref[...] = jnp.zeros(l_ref.shape, F32)
        acc_ref[...] = jnp.zeros(acc_ref.shape, F32)
        attend([(kn_ref[...], vn_ref[...], bn_ref[...])])

    attend([(kc[...].astype(BF16), vc[...].astype(BF16), bp_ref[...]) for kc, vc in zip(kc_refs, vc_refs)])

    @pl.when(p_idx == pl.num_programs(1) - 1)
    def _finish():
        o = acc_ref[...] / l_ref[...]
        o_ref[...] = _diff_finalize(o[:half], o[half:], lam_ref[0], g_ref[...], lam_init).astype(o_ref.dtype)


def _decode_bias(rows, cols, t_new, causal):
    r = jnp.arange(rows, dtype=jnp.int32)[:, None]
    c = jnp.arange(cols, dtype=jnp.int32)[None, :]
    ok = (c % N_HEADS) == ((r // t_new) % N_HEADS)
    if causal:
        ok = ok & ((c // N_HEADS) <= (r % t_new))
    return jnp.where(ok, 0.0, NEG_INF).astype(F32)


def _sample_attention(q_rows, k_new, v_new, cache_k, cache_v, page_table, lam, gain, t_new, lam_init):
    nb, rows, _ = q_rows.shape
    n_pages = page_table.shape[1]
    page_rows = cache_k.shape[1]
    new_rows = k_new.shape[1]
    bias_new = _decode_bias(rows, new_rows, t_new, True)
    bias_page = _decode_bias(rows, page_rows, t_new, False)
    n_chunk = max(c for c in DECODE_PAGES_PER_STEP if n_pages % c == 0)

    per_b = lambda r: pl.BlockSpec((None, r, HEAD_W), lambda b, p, pt: (b, 0, 0))
    fixed = lambda shape: pl.BlockSpec(shape, lambda b, p, pt: (0, 0))
    ring = (DECODE_RING, n_chunk, page_rows, HEAD_W)
    grid_spec = pltpu.PrefetchScalarGridSpec(
        num_scalar_prefetch=1,
        grid=(nb, n_pages // n_chunk),
        in_specs=[pl.BlockSpec(memory_space=pltpu.SMEM),
                  per_b(rows), per_b(new_rows), per_b(new_rows),
                  pl.BlockSpec(memory_space=pl.ANY), pl.BlockSpec(memory_space=pl.ANY),
                  fixed((rows, new_rows)), fixed((rows, page_rows)), fixed((1, V_DIM))],
        out_specs=pl.BlockSpec((None, rows // 2, V_DIM), lambda b, p, pt: (b, 0, 0)),
        scratch_shapes=[pltpu.VMEM(ring, F32), pltpu.VMEM(ring, F32), pltpu.SemaphoreType.DMA((DECODE_RING,)),
                        pltpu.VMEM((rows, V_DIM), F32), pltpu.VMEM((rows, 1), F32), pltpu.VMEM((rows, 1), F32)],
    )
    return pl.pallas_call(
        functools.partial(_decode_body, n_chunk=n_chunk, n_groups=nb * (n_pages // n_chunk), lam_init=lam_init),
        grid_spec=grid_spec,
        out_shape=jax.ShapeDtypeStruct((nb, rows // 2, V_DIM), F32),
        compiler_params=_cparams(("arbitrary", "arbitrary")),
        name="sample_attention",
    )(page_table.reshape(-1), lam, q_rows, k_new, v_new, cache_k, cache_v, bias_new, bias_page, gain)


def _pool_body(hist_ref, prev_ref, u_ref, w_ref, sc_ref, o_ref, *, tm, pos0):
    i = pl.program_id(1)
    u = u_ref[...]
    head = jnp.where(i == 0, hist_ref[...], prev_ref[...])
    ext = jnp.concatenate([head, u], axis=0)
    pos = pos0 + i * tm + lax.broadcasted_iota(jnp.int32, (tm, 1), 0)
    gdim = w_ref.shape[1]
    for g, win in enumerate(POOL_WINDOWS):
        sl = slice(g * gdim, (g + 1) * gdim)
        s = ext[:, sl]
        span = 1
        while span < win:
            s = s + pltpu.roll(s, span, axis=0)
            span *= 2
        cnt = jnp.minimum(win, pos + 1).astype(F32)
        pooled = s[HIST_ROWS:] / cnt - u[:, sl]
        mixed = jnp.dot(pooled.astype(BF16), w_ref[g], preferred_element_type=F32)
        o_ref[:, sl] = (mixed * sc_ref[:, sl]).astype(o_ref.dtype)


def _pool_mixer(hist, u, w_lin_bf, scale, batch, seq, pos0):
    d_pool = u.shape[1]
    tm = _row_tile(seq, 512)
    nt = seq // tm
    per = tm // HIST_ROWS
    return pl.pallas_call(
        functools.partial(_pool_body, tm=tm, pos0=pos0),
        grid=(batch, nt),
        in_specs=[pl.BlockSpec((None, HIST_ROWS, d_pool), lambda b, i: (b, 0, 0)),
                  pl.BlockSpec((HIST_ROWS, d_pool), lambda b, i: (jnp.maximum((b * nt + i) * per - 1, 0), 0)),
                  pl.BlockSpec((tm, d_pool), lambda b, i: (b * nt + i, 0)),
                  pl.BlockSpec(w_lin_bf.shape, lambda b, i: (0, 0, 0)),
                  pl.BlockSpec((1, d_pool), lambda b, i: (0, 0))],
        out_specs=pl.BlockSpec((tm, d_pool), lambda b, i: (b * nt + i, 0)),
        out_shape=jax.ShapeDtypeStruct(u.shape, BF16),
        compiler_params=_cparams(("arbitrary", "arbitrary")),
        name="pool_mixer",
    )(hist, u, u, w_lin_bf, scale)


def _merge_body(pool_ref, attn_ref, ga_ref, gb_ref, wp_ref, wa_ref, o_ref):
    pool_branch = jnp.dot(pool_ref[...], wp_ref[...], preferred_element_type=F32)
    attn_branch = jnp.dot(attn_ref[...], wa_ref[...], preferred_element_type=F32)
    merged = jax.nn.sigmoid(ga_ref[...]) * pool_branch + jax.nn.sigmoid(gb_ref[...]) * attn_branch
    o_ref[...] = merged.astype(o_ref.dtype)


def _gated_merge(pooled_bf, attn_bf, gates, w_pool_o_bf, w_attn_o_bf):
    n, d_pool = pooled_bf.shape
    d = w_pool_o_bf.shape[1]
    tm = _row_tile(n, 512)
    tn = 512
    nj = d // tn
    return pl.pallas_call(
        _merge_body,
        grid=(n // tm, nj),
        in_specs=[pl.BlockSpec((tm, d_pool), lambda i, j: (i, 0)),
                  pl.BlockSpec((tm, attn_bf.shape[1]), lambda i, j: (i, 0)),
                  pl.BlockSpec((tm, tn), lambda i, j: (i, j)),
                  pl.BlockSpec((tm, tn), lambda i, j: (i, nj + j)),
                  pl.BlockSpec((d_pool, tn), lambda i, j: (0, j)),
                  pl.BlockSpec((attn_bf.shape[1], tn), lambda i, j: (0, j))],
        out_specs=pl.BlockSpec((tm, tn), lambda i, j: (i, j)),
        out_shape=jax.ShapeDtypeStruct((n, d), BF16),
        compiler_params=_cparams(("arbitrary", "arbitrary")),
        name="gated_merge",
    )(pooled_bf, attn_bf, gates, gates, w_pool_o_bf, w_attn_o_bf)


def _layer_norm(y, g, b):
    mu = jnp.mean(y, axis=-1, keepdims=True)
    yc = y - mu
    var = jnp.mean(yc * yc, axis=-1, keepdims=True)
    return yc * lax.rsqrt(var + LN_EPS) * g + b


def _mix_ln_body(m_ref, w_ref, x_ref, g_ref, b_ref, rh_ref, rl_ref, *rest, alpha, n_blocks):
    x1_ref, lg_ref = rest[-2:]
    i = pl.program_id(0)

    @pl.when(i < n_blocks)
    def _rows():
        mix = jnp.dot(m_ref[...], w_ref[...], preferred_element_type=F32)
        x1 = _layer_norm(alpha * x_ref[...] + mix, g_ref[...], b_ref[...])
        x1_ref[...] = x1
        hi = x1.astype(BF16)
        lo = (x1 - hi.astype(F32)).astype(BF16)
        lg_ref[...] = (jnp.dot(hi, rh_ref[...], preferred_element_type=F32)
                       + jnp.dot(lo, rh_ref[...], preferred_element_type=F32)
                       + jnp.dot(hi, rl_ref[...], preferred_element_type=F32))

    if len(rest) == 3:
        tail_ref = rest[0]

        @pl.when(i == n_blocks)
        def _tail():
            n_tail = tail_ref.shape[0]
            x1_ref[:n_tail, :] = tail_ref[...]
            if n_tail < x1_ref.shape[0]:
                x1_ref[n_tail:, :] = jnp.zeros((x1_ref.shape[0] - n_tail, x1_ref.shape[1]), x1_ref.dtype)


def _mix_ln_router(merged_bf, w_out_bf, x, ln_g, ln_b, r_hi, r_lo, alpha, tail=None):
    n, d = x.shape
    tm = _row_tile(n, 256)
    nb = n // tm
    row = lambda i: (jnp.minimum(i, nb - 1), 0)
    fixed = lambda i: (0, 0)
    in_specs = [pl.BlockSpec((tm, d), row), pl.BlockSpec((d, d), fixed), pl.BlockSpec((tm, d), row),
                pl.BlockSpec((1, d), fixed), pl.BlockSpec((1, d), fixed),
                pl.BlockSpec((d, ROUTER_PAD), fixed), pl.BlockSpec((d, ROUTER_PAD), fixed)]
    args = [merged_bf, w_out_bf, x, ln_g, ln_b, r_hi, r_lo]
    extra = 0
    if tail is not None:
        assert tail.shape[0] <= tm and tail.shape[0] % SUBLANES == 0
        in_specs.append(pl.BlockSpec(tail.shape, fixed))
        args.append(tail)
        extra = 1
    return pl.pallas_call(
        functools.partial(_mix_ln_body, alpha=alpha, n_blocks=nb),
        grid=(nb + extra,),
        in_specs=in_specs,
        out_specs=(pl.BlockSpec((tm, d), lambda i: (i, 0)), pl.BlockSpec((tm, ROUTER_PAD), row)),
        out_shape=(jax.ShapeDtypeStruct((n + extra * tm, d), F32), jax.ShapeDtypeStruct((n, ROUTER_PAD), F32)),
        compiler_params=_cparams(("arbitrary",)),
        name="mix_ln_router",
    )(*args)


def _row_gather(idx_ref, base, count, src_hbm, dst_buf, sem):
    def start(r, carry):
        pltpu.make_async_copy(src_hbm.at[pl.ds(idx_ref[base + r], 1)], dst_buf.at[pl.ds(r, 1)], sem).start()
        return carry

    lax.fori_loop(0, count, start, 0, unroll=8)


def _row_gather_wait(count, src_hbm, dst_buf, sem):
    pltpu.make_async_copy(src_hbm.at[pl.ds(0, count)], dst_buf, sem).wait()


def _moe_body(te_ref, nt_ref, rt_ref, x_hbm, rw_ref, wg_ref, wu_ref, wd_ref, o_ref, xbuf, sems, *, tm):
    del te_ref
    t = pl.program_id(0)
    n_used = nt_ref[0]

    def fetch(tile, slot):
        _row_gather(rt_ref, tile * tm, tm, x_hbm, xbuf.at[slot], sems.at[slot])

    @pl.when(t == 0)
    def _prime():
        fetch(0, 0)

    @pl.when(t + 1 < n_used)
    def _prefetch():
        fetch(t + 1, (t + 1) % 2)

    @pl.when(t < n_used)
    def _run():
        slot = t % 2
        _row_gather_wait(tm, x_hbm, xbuf.at[slot], sems.at[slot])
        x = xbuf[slot].astype(BF16)
        gate = jnp.dot(x, wg_ref[...].astype(BF16), preferred_element_type=F32)
        up = jnp.dot(x, wu_ref[...].astype(BF16), preferred_element_type=F32)
        h = (gate * jax.nn.sigmoid(gate)) * up
        y = jnp.dot(h.astype(BF16), wd_ref[...].astype(BF16), preferred_element_type=F32)
        o_ref[...] = y * rw_ref[...]

    @pl.when(t >= n_used)
    def _skip():
        o_ref[...] = jnp.zeros(o_ref.shape, o_ref.dtype)


def _moe_experts(tile_expert, n_tiles_used, row_token, x_all, row_w, w_gate, w_up, w_down, tm):
    r = row_token.shape[0]
    d = x_all.shape[1]
    de = w_gate.shape[2]
    n_tiles = r // tm
    grid_spec = pltpu.PrefetchScalarGridSpec(
        num_scalar_prefetch=3,
        grid=(n_tiles,),
        in_specs=[pl.BlockSpec(memory_space=pl.ANY),
                  pl.BlockSpec((tm, 1), lambda t, te, nt, rt: (t, 0)),
                  pl.BlockSpec((None, d, de), lambda t, te, nt, rt: (te[t], 0, 0)),
                  pl.BlockSpec((None, d, de), lambda t, te, nt, rt: (te[t], 0, 0)),
                  pl.BlockSpec((None, de, d), lambda t, te, nt, rt: (te[t], 0, 0))],
        out_specs=pl.BlockSpec((tm, d), lambda t, te, nt, rt: (t, 0)),
        scratch_shapes=[pltpu.VMEM((2, tm, d), F32), pltpu.SemaphoreType.DMA((2,))],
    )
    return pl.pallas_call(
        functools.partial(_moe_body, tm=tm),
        grid_spec=grid_spec,
        out_shape=jax.ShapeDtypeStruct((r, d), F32),
        compiler_params=_cparams(("arbitrary",)),
        name="moe_experts",
    )(tile_expert, n_tiles_used, row_token, x_all, row_w, w_gate, w_up, w_down)


def _route(logits, b_rg, b_re):
    g_logits = logits[:, :N_GROUPS] + b_rg
    g_prob = jax.nn.softmax(g_logits, -1)
    g_sel = jnp.argmax(g_logits, -1)
    g_w = jnp.take_along_axis(g_prob, g_sel[:, None], -1)
    e_logits = (logits[:, N_GROUPS:N_GROUPS + N_EXPERTS] + b_re).reshape(-1, N_GROUPS, EXPERTS_PER_GROUP)
    e_logits = jnp.take_along_axis(e_logits, g_sel[:, None, None], axis=1)[:, 0]
    top_v, top_i = lax.top_k(e_logits, TOP_K)
    e_w = jax.nn.softmax(top_v, -1) * g_w
    e_id = g_sel[:, None] * EXPERTS_PER_GROUP + top_i
    return e_id.astype(jnp.int32), e_w


def _dispatch_plan(e_id, e_w, tm):
    n = e_id.shape[0]
    flat_e = e_id.reshape(-1)
    n_assign = flat_e.shape[0]
    onehot = (flat_e[:, None] == jnp.arange(N_EXPERTS, dtype=jnp.int32)[None, :]).astype(jnp.int32)
    ranks = jnp.cumsum(onehot, axis=0) - onehot
    rank = jnp.sum(ranks * onehot, axis=1)
    counts = jnp.sum(onehot, axis=0)
    padded = ((counts + tm - 1) // tm) * tm
    ends = jnp.cumsum(padded)
    starts = ends - padded
    pos = starts[flat_e] + rank
    n_tiles = (n_assign + tm - 1) // tm + N_EXPERTS
    rows = n_tiles * tm
    row_assign = jnp.zeros((rows,), jnp.int32).at[pos].set(jnp.arange(1, n_assign + 1, dtype=jnp.int32))
    live = row_assign > 0
    row_assign = jnp.maximum(row_assign - 1, 0)
    row_token = row_assign // TOP_K
    row_w = jnp.where(live, e_w.reshape(-1)[row_assign], 0.0)
    tile_start = jnp.arange(n_tiles, dtype=jnp.int32) * tm
    tile_expert = jnp.sum((tile_start[:, None] >= ends[None, :]).astype(jnp.int32), axis=1)
    n_used = (ends[-1] // tm).astype(jnp.int32)
    last_expert = jnp.sum((ends[-1] - 1 >= ends).astype(jnp.int32))
    tile_expert = jnp.minimum(tile_expert, last_expert).astype(jnp.int32)
    return row_token, row_w.reshape(rows, 1), tile_expert, n_used.reshape(1), pos.reshape(n, TOP_K)


def _ln2_body(pos_ref, x_ref, y_hbm, g_ref, b_ref, op_ref, os_ref, ybuf, sems, *, tm, alpha, prompt_blocks):
    i = pl.program_id(0)
    rows = TOP_K * tm

    def fetch(blk, slot):
        _row_gather(pos_ref, blk * rows, rows, y_hbm, ybuf.at[slot], sems.at[slot])

    @pl.when(i == 0)
    def _prime():
        fetch(0, 0)

    @pl.when(i + 1 < pl.num_programs(0))
    def _prefetch():
        fetch(i + 1, (i + 1) % 2)

    slot = i % 2
    _row_gather_wait(rows, y_hbm, ybuf.at[slot], sems.at[slot])
    moe = functools.reduce(jnp.add, [ybuf[slot, k * tm:(k + 1) * tm] for k in range(TOP_K)])
    out = _layer_norm(alpha * x_ref[...] + moe, g_ref[...], b_ref[...])

    @pl.when(i < prompt_blocks)
    def _prompt():
        op_ref[...] = out

    @pl.when(i >= prompt_blocks)
    def _sample():
        os_ref[...] = out


def _combine_ln(pos_blocked, x1_all, y_sorted, ln_g, ln_b, alpha, n_prompt, n, tm):
    d = x1_all.shape[1]
    assert n % tm == 0 and n_prompt % tm == 0 and n_prompt < n <= x1_all.shape[0]
    pb = n_prompt // tm
    grid_spec = pltpu.PrefetchScalarGridSpec(
        num_scalar_prefetch=1,
        grid=(n // tm,),
        in_specs=[pl.BlockSpec((tm, d), lambda i, pos: (i, 0)),
                  pl.BlockSpec(memory_space=pl.ANY),
                  pl.BlockSpec((1, d), lambda i, pos: (0, 0)), pl.BlockSpec((1, d), lambda i, pos: (0, 0))],
        out_specs=(pl.BlockSpec((tm, d), lambda i, pos: (jnp.minimum(i, pb - 1), 0)),
                   pl.BlockSpec((tm, d), lambda i, pos: (jnp.maximum(i - pb, 0), 0))),
        scratch_shapes=[pltpu.VMEM((2, TOP_K * tm, d), F32), pltpu.SemaphoreType.DMA((2,))],
    )
    return pl.pallas_call(
        functools.partial(_ln2_body, tm=tm, alpha=alpha, prompt_blocks=pb),
        grid_spec=grid_spec,
        out_shape=(jax.ShapeDtypeStruct((n_prompt, d), F32), jax.ShapeDtypeStruct((n - n_prompt, d), F32)),
        compiler_params=_cparams(("arbitrary",)),
        name="combine_ln",
    )(pos_blocked, x1_all, y_sorted, ln_g, ln_b)


def kernel(x_prompt, x_sample, cache_k, cache_v, state_pool, page_table, w_in, lambda_q1, lambda_k1, lambda_q2, lambda_k2, subln_g, w_pool_lin, pool_scale, w_pool_o, w_attn_o, w_out, ln1_g, ln1_b, w_router_group, b_router_group, w_router_expert, b_router_expert, w_exp_gate, w_exp_up, w_exp_down, ln2_g, ln2_b):
    depth = w_in.shape[0]
    assert depth == 1, "one layer per step"
    layer = 0
    batch, seq, d_model = x_prompt.shape
    dec_batch, dec_seq, _ = x_sample.shape
    n_pages = page_table.shape[1]
    page_size = cache_k.shape[2]
    d_pool = w_pool_lin.shape[1] * w_pool_lin.shape[2]
    past_len = n_pages * page_size
    alpha = (2.0 * depth) ** 0.25
    lam_init = 0.8 - 0.6 * math.exp(-0.3 * layer)
    n_p, n_s = batch * seq, dec_batch * dec_seq

    lam = (jnp.exp(jnp.sum(lambda_q1[layer].astype(F32) * lambda_k1[layer].astype(F32)))
           - jnp.exp(jnp.sum(lambda_q2[layer].astype(F32) * lambda_k2[layer].astype(F32))) + lam_init).reshape(1)
    gain = subln_g[layer].astype(F32).reshape(1, V_DIM)

    w_in_bf = w_in[layer].astype(BF16)
    w_lin_bf = w_pool_lin[layer].astype(BF16)
    scale = pool_scale[layer].astype(F32).reshape(1, d_pool)
    w_pool_o_bf = w_pool_o[layer].astype(BF16)
    w_attn_o_bf = w_attn_o[layer].astype(BF16)
    w_out_bf = w_out[layer].astype(BF16)
    ln1g, ln1b = ln1_g[layer].reshape(1, d_model), ln1_b[layer].reshape(1, d_model)
    ln2g, ln2b = ln2_g[layer].reshape(1, d_model), ln2_b[layer].reshape(1, d_model)
    w_router = jnp.concatenate([w_router_group[layer], w_router_expert[layer]], axis=1).astype(F32)
    w_router = jnp.pad(w_router, ((0, 0), (0, ROUTER_PAD - w_router.shape[1])))
    r_hi = w_router.astype(BF16)
    r_lo = (w_router - r_hi.astype(F32)).astype(BF16)

    xp = x_prompt.reshape(n_p, d_model)
    q_p, k_p, kb_p, v_p, vb_p, u_p, g_p = _in_projection(xp, w_in_bf, d_pool)
    attn_p = _prompt_attention(q_p, kb_p, vb_p, lam, gain.reshape(V_DIM, 1), batch, seq, lam_init)
    pooled_p = _pool_mixer(jnp.zeros((batch, HIST_ROWS, d_pool), F32), u_p, w_lin_bf, scale, batch, seq, 0)
    merged_p = _gated_merge(pooled_p, attn_p, g_p, w_pool_o_bf, w_attn_o_bf)

    xs = x_sample.reshape(n_s, d_model)
    q_s, k_s, kb_s, v_s, vb_s, u_s, g_s = _in_projection(xs, w_in_bf, d_pool)
    q5 = q_s.reshape(dec_batch, dec_seq, N_HEADS, 2, QK_DIM).transpose(0, 3, 2, 1, 4)
    zq = jnp.zeros_like(q5[:, 0])
    q_rows = jnp.stack([jnp.concatenate([q5[:, 0], zq], -1), jnp.concatenate([zq, q5[:, 1]], -1)], axis=1)
    q_rows = q_rows.reshape(dec_batch, 2 * N_HEADS * dec_seq, HEAD_W)
    new_rows = LANES
    assert dec_seq * N_HEADS <= new_rows
    pad_new = lambda a: jnp.pad(a.reshape(dec_batch, dec_seq * N_HEADS, HEAD_W),
                                ((0, 0), (0, new_rows - dec_seq * N_HEADS), (0, 0)))
    ck = cache_k[layer].reshape(cache_k.shape[1], page_size * N_HEADS, HEAD_W)
    cv = cache_v[layer].reshape(cache_v.shape[1], page_size * N_HEADS, V_DIM)
    attn_s = _sample_attention(q_rows, pad_new(kb_s), pad_new(vb_s), ck, cv, page_table.astype(jnp.int32),
                               lam, gain, dec_seq, lam_init)
    attn_s = attn_s.reshape(dec_batch, N_HEADS, dec_seq, V_DIM).transpose(0, 2, 1, 3).reshape(n_s, N_HEADS * V_DIM)
    seq_pad = -(-dec_seq // SUBLANES) * SUBLANES
    hist_s = jnp.pad(state_pool[layer].astype(F32), ((0, 0), (HIST_ROWS - POOL_HIST, 0), (0, 0)))
    u_s3 = u_s.reshape(dec_batch, dec_seq, d_pool)
    u_pad = jnp.pad(u_s3, ((0, 0), (0, seq_pad - dec_seq), (0, 0))).reshape(dec_batch * seq_pad, d_pool)
    pooled_s = _pool_mixer(hist_s, u_pad, w_lin_bf, scale, dec_batch, seq_pad, past_len)
    pooled_s = pooled_s.reshape(dec_batch, seq_pad, d_pool)[:, :dec_seq].reshape(n_s, d_pool)
    merged_s = _gated_merge(pooled_s, attn_s.astype(BF16), g_s, w_pool_o_bf, w_attn_o_bf)
    x1_s, lg_s = _mix_ln_router(merged_s, w_out_bf, xs, ln1g, ln1b, r_hi, r_lo, alpha)
    x1_all, lg_p = _mix_ln_router(merged_p, w_out_bf, xp, ln1g, ln1b, r_hi, r_lo, alpha, tail=x1_s)

    tm_moe = 256
    logits = jnp.concatenate([lg_p, lg_s], axis=0)
    e_id, e_w = _route(logits, b_router_group[layer].astype(F32), b_router_expert[layer].astype(F32))
    row_token, row_w, tile_expert, n_used, pos = _dispatch_plan(e_id, e_w, tm_moe)
    y_sorted = _moe_experts(tile_expert, n_used, row_token, x1_all, row_w, w_exp_gate[layer], w_exp_up[layer],
                            w_exp_down[layer], tm_moe)
    tm_ln = _row_tile(math.gcd(n_p, n_s), 128)
    pos_blocked = pos.reshape(-1, tm_ln, TOP_K).transpose(0, 2, 1).reshape(-1)
    y_p, y_s = _combine_ln(pos_blocked, x1_all, y_sorted, ln2g, ln2b, alpha, n_p, n_p + n_s, tm_ln)

    assert seq >= POOL_HIST
    new_pool_p = u_p.reshape(batch, seq, d_pool)[:, seq - POOL_HIST:]
    new_pool_s = jnp.concatenate([state_pool[layer].astype(F32), u_s3], axis=1)[:, -POOL_HIST:]
    return (y_p.reshape(batch, seq, d_model), y_s.reshape(dec_batch, dec_seq, d_model),
            k_p.reshape(1, batch, seq, N_HEADS, HEAD_W), v_p.reshape(1, batch, seq, N_HEADS, V_DIM),
            new_pool_p[None],
            k_s.reshape(1, dec_batch, dec_seq, N_HEADS, HEAD_W), v_s.reshape(1, dec_batch, dec_seq, N_HEADS, V_DIM),
            new_pool_s[None])
```

```python
import functools
import math

import jax
import jax.numpy as jnp
from jax import lax
from jax.experimental import pallas as pl
from jax.experimental.pallas import tpu as pltpu

F32 = jnp.float32
BF16 = jnp.bfloat16

N_HEADS = 8
QK_DIM = 128
V_DIM = 2 * QK_DIM
HEAD_W = 2 * QK_DIM
ATTN_SCALE = QK_DIM ** -0.5
LOG2_E = math.log2(math.e)
POOL_WINDOWS = (2, 4, 8, 16)
N_POOL_GROUPS = len(POOL_WINDOWS)
POOL_HIST = max(POOL_WINDOWS) - 1
HIST_ROWS = POOL_HIST + 1
N_GROUPS = 4
EXPERTS_PER_GROUP = 8
N_EXPERTS = N_GROUPS * EXPERTS_PER_GROUP
TOP_K = 2
LN_EPS = 1e-5

LANES = 128
SUBLANES = 8
VMEM_LIMIT_BYTES = 56 * 1024 * 1024
ROUTER_PAD = LANES
DECODE_PAGES_PER_STEP = (4, 2, 1)
DECODE_RING = 3

NEG_INF = float("-inf")


def _cparams(sem):
    return pltpu.CompilerParams(dimension_semantics=sem, vmem_limit_bytes=VMEM_LIMIT_BYTES)


def _row_tile(n, pref):
    t = min(pref, n)
    while n % t:
        t //= 2
    assert t >= SUBLANES, (n, pref)
    return t


def _proj_body(x_ref, w_ref, q_ref, k_ref, kb_ref, v_ref, vb_ref, u_ref, g_ref, xb_ref, *, bounds):
    j = pl.program_id(1)

    @pl.when(j == 0)
    def _cast():
        xb_ref[...] = x_ref[...].astype(BF16)

    acc = jnp.dot(xb_ref[...], w_ref[...], preferred_element_type=F32)
    q_hi, k_hi, v_hi, u_hi = bounds

    @pl.when(j < q_hi)
    def _q():
        q_ref[...] = (acc * (ATTN_SCALE * LOG2_E)).astype(BF16)

    @pl.when((j >= q_hi) & (j < k_hi))
    def _k():
        k_ref[...] = acc
        kb_ref[...] = acc.astype(BF16)

    @pl.when((j >= k_hi) & (j < v_hi))
    def _v():
        v_ref[...] = acc
        vb_ref[...] = acc.astype(BF16)

    @pl.when((j >= v_hi) & (j < u_hi))
    def _u():
        u_ref[...] = acc

    @pl.when(j >= u_hi)
    def _g():
        g_ref[...] = acc


def _in_projection(x, w_bf, d_pool):
    n, d = x.shape
    qw = N_HEADS * HEAD_W
    gw = w_bf.shape[1] - 3 * qw - d_pool
    tn = 512
    tm = _row_tile(n, 1024)
    bounds = (qw // tn, 2 * qw // tn, 3 * qw // tn, (3 * qw + d_pool) // tn)
    q_hi, k_hi, v_hi, u_hi = bounds
    n_col = w_bf.shape[1] // tn

    def seg(lo, hi):
        return lambda i, j: (i, jnp.clip(j - lo, 0, hi - lo - 1))

    out_shape = (
        jax.ShapeDtypeStruct((n, qw), BF16),
        jax.ShapeDtypeStruct((n, qw), F32), jax.ShapeDtypeStruct((n, qw), BF16),
        jax.ShapeDtypeStruct((n, qw), F32), jax.ShapeDtypeStruct((n, qw), BF16),
        jax.ShapeDtypeStruct((n, d_pool), F32),
        jax.ShapeDtypeStruct((n, gw), F32),
    )
    blk = lambda lo, hi: pl.BlockSpec((tm, tn), seg(lo, hi))
    return pl.pallas_call(
        functools.partial(_proj_body, bounds=bounds),
        grid=(n // tm, n_col),
        in_specs=[pl.BlockSpec((tm, d), lambda i, j: (i, 0)),
                  pl.BlockSpec((d, tn), lambda i, j: (0, j))],
        out_specs=(blk(0, q_hi), blk(q_hi, k_hi), blk(q_hi, k_hi), blk(k_hi, v_hi), blk(k_hi, v_hi),
                   blk(v_hi, u_hi), blk(u_hi, n_col)),
        out_shape=out_shape,
        scratch_shapes=[pltpu.VMEM((tm, d), BF16)],
        compiler_params=_cparams(("arbitrary", "arbitrary")),
        name="in_projection",
    )(x, w_bf)


def _diff_finalize(o1, o2, lam, gain, lam_init):
    o = o1 - lam * o2
    o = o * lax.rsqrt(jnp.mean(o * o, axis=-1, keepdims=True) + LN_EPS)
    return o * gain * (1.0 - lam_init)


def _flash_body(lam_ref, q_ref, k_ref, v_ref, g_ref, o_ref, qbd_ref, s0_ref, s1_ref, acc_ref, m_ref, l_ref,
                *, tq, lam_init):
    i = pl.program_id(2)
    q = q_ref[...]
    zeros = jnp.zeros((tq, QK_DIM), q.dtype)
    qbd_ref[:tq, :] = jnp.concatenate([q[:, :QK_DIM], zeros], axis=1)
    qbd_ref[tq:, :] = jnp.concatenate([zeros, q[:, QK_DIM:]], axis=1)
    m_ref[...] = jnp.full(m_ref.shape, NEG_INF, F32)
    l_ref[...] = jnp.zeros(l_ref.shape, F32)
    acc_ref[...] = jnp.zeros(acc_ref.shape, F32)

    def scores(j, dst_ref):
        ks = k_ref[pl.ds(pl.multiple_of(j * tq, tq), tq), :]
        dst_ref[...] = lax.dot_general(ks, qbd_ref[...], (((1,), (1,)), ((), ())), preferred_element_type=F32)

    def block(j, src_ref, masked):
        vs = v_ref[pl.ds(pl.multiple_of(j * tq, tq), tq), :]
        s = src_ref[...]
        if masked:
            key = lax.broadcasted_iota(jnp.int32, (tq, tq), 0)
            qry = lax.broadcasted_iota(jnp.int32, (tq, tq), 1)
            visible = key <= qry
            s = jnp.concatenate([jnp.where(visible, s[:, :tq], NEG_INF), jnp.where(visible, s[:, tq:], NEG_INF)],
                                axis=1)
        m_old = m_ref[...]
        m_new = jnp.maximum(m_old, jnp.max(s, axis=0, keepdims=True))
        p = jnp.exp2(s - m_new)
        a = jnp.exp2(m_old - m_new)
        l_ref[...] = a * l_ref[...] + jnp.sum(p, axis=0, keepdims=True)
        pv = lax.dot_general(vs, p.astype(BF16), (((0,), (0,)), ((), ())), preferred_element_type=F32)
        acc_ref[...] = a * acc_ref[...] + pv
        m_ref[...] = m_new

    def off_diag_pair(pair, c):
        j = 2 * pair
        scores(j + 1, s1_ref)
        block(j, s0_ref, False)
        scores(j + 2, s0_ref)
        block(j + 1, s1_ref, False)
        return c

    scores(0, s0_ref)
    lax.fori_loop(0, i // 2, off_diag_pair, 0)

    @pl.when(i % 2 == 0)
    def _diag_even():
        block(i, s0_ref, True)

    @pl.when(i % 2 == 1)
    def _diag_odd():
        scores(i, s1_ref)
        block(i - 1, s0_ref, False)
        block(i, s1_ref, True)

    o_t = acc_ref[...] / l_ref[...]
    o = o_t[:, :tq] - lam_ref[0] * o_t[:, tq:]
    o = o * lax.rsqrt(jnp.mean(o * o, axis=0, keepdims=True) + LN_EPS) * g_ref[...] * (1.0 - lam_init)
    o_ref[...] = o.T.astype(o_ref.dtype)


def _prompt_attention(q_bf, k_bf, v_bf, lam, gain_col, batch, seq, lam_init):
    tq = _row_tile(seq, 512)
    nq = seq // tq
    kv_spec = pl.BlockSpec((seq, HEAD_W), lambda b, h, i: (b, h))
    return pl.pallas_call(
        functools.partial(_flash_body, tq=tq, lam_init=lam_init),
        grid=(batch, N_HEADS, nq),
        in_specs=[pl.BlockSpec(memory_space=pltpu.SMEM),
                  pl.BlockSpec((tq, HEAD_W), lambda b, h, i: (b * nq + i, h)),
                  kv_spec, kv_spec,
                  pl.BlockSpec((V_DIM, 1), lambda b, h, i: (0, 0))],
        out_specs=pl.BlockSpec((tq, V_DIM), lambda b, h, i: (b * nq + i, h)),
        out_shape=jax.ShapeDtypeStruct((batch * seq, N_HEADS * V_DIM), BF16),
        scratch_shapes=[pltpu.VMEM((2 * tq, HEAD_W), BF16),
                        pltpu.VMEM((tq, 2 * tq), F32), pltpu.VMEM((tq, 2 * tq), F32),
                        pltpu.VMEM((V_DIM, 2 * tq), F32),
                        pltpu.VMEM((1, 2 * tq), F32), pltpu.VMEM((1, 2 * tq), F32)],
        compiler_params=_cparams(("arbitrary", "arbitrary", "arbitrary")),
        name="prompt_attention",
    )(lam, q_bf, k_bf, v_bf, gain_col)


def _decode_body(pt_ref, lam_ref, q_ref, kn_ref, vn_ref, kc_hbm, vc_hbm, bn_ref, bp_ref, g_ref, o_ref,
                 kbuf, vbuf, sems, acc_ref, m_ref, l_ref, *, n_chunk, n_groups, lam_init):
    p_idx = pl.program_id(1)
    g = pl.program_id(0) * pl.num_programs(1) + p_idx
    half = q_ref.shape[0] // 2
    q = q_ref[...]

    def page_copies(group, slot):
        copies = []
        for r in range(n_chunk):
            page = pt_ref[group * n_chunk + r]
            copies.append(pltpu.make_async_copy(kc_hbm.at[page], kbuf.at[slot, r], sems.at[slot]))
            copies.append(pltpu.make_async_copy(vc_hbm.at[page], vbuf.at[slot, r], sems.at[slot]))
        return copies

    def start_group(group):
        for cp in page_copies(group, group % DECODE_RING):
            cp.start()

    @pl.when(g == 0)
    def _prime():
        for ahead in range(min(DECODE_RING - 1, n_groups)):
            start_group(ahead)

    @pl.when(g + (DECODE_RING - 1) < n_groups)
    def _prefetch():
        start_group(g + (DECODE_RING - 1))

    slot = g % DECODE_RING
    for cp in page_copies(g, slot):
        cp.wait()
    kc_refs = [kbuf.at[slot, r] for r in range(n_chunk)]
    vc_refs = [vbuf.at[slot, r] for r in range(n_chunk)]

    def attend(blocks):
        scores = [lax.dot_general(q, k_bf, (((1,), (1,)), ((), ())), preferred_element_type=F32) + bias
                  for k_bf, _, bias in blocks]
        m_old = m_ref[...]
        m_new = jnp.maximum(m_old, jnp.max(functools.reduce(jnp.maximum, scores), axis=-1, keepdims=True))
        probs = [jnp.exp2(s - m_new) for s in scores]
        a = jnp.exp2(m_old - m_new)
        l_ref[...] = a * l_ref[...] + jnp.sum(functools.reduce(jnp.add, probs), axis=-1, keepdims=True)
        pv = [jnp.dot(p.astype(BF16), v_bf, preferred_element_type=F32) for p, (_, v_bf, _) in zip(probs, blocks)]
        acc_ref[...] = a * acc_ref[...] + functools.reduce(jnp.add, pv)
        m_ref[...] = m_new

    @pl.when(p_idx == 0)
    def _new_tokens():
        m_ref[...] = jnp.full(m_ref.shape, NEG_INF, F32)
        l_ref[...] = jnp.zeros(l_ref.shape, F32)
        acc_ref[...] = jnp.zeros(acc_ref.shape, F32)
        attend([(kn_ref[...], vn_ref[...], bn_ref[...])])

    attend([(kc[...].astype(BF16), vc[...].astype(BF16), bp_ref[...]) for kc, vc in zip(kc_refs, vc_refs)])

    @pl.when(p_idx == pl.num_programs(1) - 1)
    def _finish():
        o = acc_ref[...] / l_ref[...]
        o_ref[...] = _diff_finalize(o[:half], o[half:], lam_ref[0], g_ref[...], lam_init).astype(o_ref.dtype)


def _decode_bias(rows, cols, t_new, causal):
    r = jnp.arange(rows, dtype=jnp.int32)[:, None]
    c = jnp.arange(cols, dtype=jnp.int32)[None, :]
    ok = (c % N_HEADS) == ((r // t_new) % N_HEADS)
    if causal:
        ok = ok & ((c // N_HEADS) <= (r % t_new))
    return jnp.where(ok, 0.0, NEG_INF).astype(F32)


def _sample_attention(q_rows, k_new, v_new, cache_k, cache_v, page_table, lam, gain, t_new, lam_init):
    nb, rows, _ = q_rows.shape
    n_pages = page_table.shape[1]
    page_rows = cache_k.shape[1]
    new_rows = k_new.shape[1]
    bias_new = _decode_bias(rows, new_rows, t_new, True)
    bias_page = _decode_bias(rows, page_rows, t_new, False)
    n_chunk = max(c for c in DECODE_PAGES_PER_STEP if n_pages % c == 0)

    per_b = lambda r: pl.BlockSpec((None, r, HEAD_W), lambda b, p, pt: (b, 0, 0))
    fixed = lambda shape: pl.BlockSpec(shape, lambda b, p, pt: (0, 0))
    ring = (DECODE_RING, n_chunk, page_rows, HEAD_W)
    grid_spec = pltpu.PrefetchScalarGridSpec(
        num_scalar_prefetch=1,
        grid=(nb, n_pages // n_chunk),
        in_specs=[pl.BlockSpec(memory_space=pltpu.SMEM),
                  per_b(rows), per_b(new_rows), per_b(new_rows),
                  pl.BlockSpec(memory_space=pl.ANY), pl.BlockSpec(memory_space=pl.ANY),
                  fixed((rows, new_rows)), fixed((rows, page_rows)), fixed((1, V_DIM))],
        out_specs=pl.BlockSpec((None, rows // 2, V_DIM), lambda b, p, pt: (b, 0, 0)),
        scratch_shapes=[pltpu.VMEM(ring, F32), pltpu.VMEM(ring, F32), pltpu.SemaphoreType.DMA((DECODE_RING,)),
                        pltpu.VMEM((rows, V_DIM), F32), pltpu.VMEM((rows, 1), F32), pltpu.VMEM((rows, 1), F32)],
    )
    return pl.pallas_call(
        functools.partial(_decode_body, n_chunk=n_chunk, n_groups=nb * (n_pages // n_chunk), lam_init=lam_init),
        grid_spec=grid_spec,
        out_shape=jax.ShapeDtypeStruct((nb, rows // 2, V_DIM), F32),
        compiler_params=_cparams(("arbitrary", "arbitrary")),
        name="sample_attention",
    )(page_table.reshape(-1), lam, q_rows, k_new, v_new, cache_k, cache_v, bias_new, bias_page, gain)


def _pool_body(hist_ref, prev_ref, u_ref, w_ref, sc_ref, o_ref, *, tm, pos0):
    i = pl.program_id(1)
    u = u_ref[...]
    head = jnp.where(i == 0, hist_ref[...], prev_ref[...])
    ext = jnp.concatenate([head, u], axis=0)
    pos = pos0 + i * tm + lax.broadcasted_iota(jnp.int32, (tm, 1), 0)
    gdim = w_ref.shape[1]
    for g, win in enumerate(POOL_WINDOWS):
        sl = slice(g * gdim, (g + 1) * gdim)
        s = ext[:, sl]
        span = 1
        while span < win:
            s = s + pltpu.roll(s, span, axis=0)
            span *= 2
        cnt = jnp.minimum(win, pos + 1).astype(F32)
        pooled = s[HIST_ROWS:] / cnt - u[:, sl]
        mixed = jnp.dot(pooled.astype(BF16), w_ref[g], preferred_element_type=F32)
        o_ref[:, sl] = (mixed * sc_ref[:, sl]).astype(o_ref.dtype)


def _pool_mixer(hist, u, w_lin_bf, scale, batch, seq, pos0):
    d_pool = u.shape[1]
    tm = _row_tile(seq, 512)
    nt = seq // tm
    per = tm // HIST_ROWS
    return pl.pallas_call(
        functools.partial(_pool_body, tm=tm, pos0=pos0),
        grid=(batch, nt),
        in_specs=[pl.BlockSpec((None, HIST_ROWS, d_pool), lambda b, i: (b, 0, 0)),
                  pl.BlockSpec((HIST_ROWS, d_pool), lambda b, i: (jnp.maximum((b * nt + i) * per - 1, 0), 0)),
                  pl.BlockSpec((tm, d_pool), lambda b, i: (b * nt + i, 0)),
                  pl.BlockSpec(w_lin_bf.shape, lambda b, i: (0, 0, 0)),
                  pl.BlockSpec((1, d_pool), lambda b, i: (0, 0))],
        out_specs=pl.BlockSpec((tm, d_pool), lambda b, i: (b * nt + i, 0)),
        out_shape=jax.ShapeDtypeStruct(u.shape, BF16),
        compiler_params=_cparams(("arbitrary", "arbitrary")),
        name="pool_mixer",
    )(hist, u, u, w_lin_bf, scale)


def _merge_body(pool_ref, attn_ref, ga_ref, gb_ref, wp_ref, wa_ref, o_ref):
    pool_branch = jnp.dot(pool_ref[...], wp_ref[...], preferred_element_type=F32)
    attn_branch = jnp.dot(attn_ref[...], wa_ref[...], preferred_element_type=F32)
    merged = jax.nn.sigmoid(ga_ref[...]) * pool_branch + jax.nn.sigmoid(gb_ref[...]) * attn_branch
    o_ref[...] = merged.astype(o_ref.dtype)


def _gated_merge(pooled_bf, attn_bf, gates, w_pool_o_bf, w_attn_o_bf):
    n, d_pool = pooled_bf.shape
    d = w_pool_o_bf.shape[1]
    tm = _row_tile(n, 1024)
    tn = 512
    nj = d // tn
    return pl.pallas_call(
        _merge_body,
        grid=(n // tm, nj),
        in_specs=[pl.BlockSpec((tm, d_pool), lambda i, j: (i, 0)),
                  pl.BlockSpec((tm, attn_bf.shape[1]), lambda i, j: (i, 0)),
                  pl.BlockSpec((tm, tn), lambda i, j: (i, j)),
                  pl.BlockSpec((tm, tn), lambda i, j: (i, nj + j)),
                  pl.BlockSpec((d_pool, tn), lambda i, j: (0, j)),
                  pl.BlockSpec((attn_bf.shape[1], tn), lambda i, j: (0, j))],
        out_specs=pl.BlockSpec((tm, tn), lambda i, j: (i, j)),
        out_shape=jax.ShapeDtypeStruct((n, d), BF16),
        compiler_params=_cparams(("arbitrary", "arbitrary")),
        name="gated_merge",
    )(pooled_bf, attn_bf, gates, gates, w_pool_o_bf, w_attn_o_bf)


def _layer_norm(y, g, b):
    mu = jnp.mean(y, axis=-1, keepdims=True)
    yc = y - mu
    var = jnp.mean(yc * yc, axis=-1, keepdims=True)
    return yc * lax.rsqrt(var + LN_EPS) * g + b


def _mix_ln_body(m_ref, w_ref, x_ref, g_ref, b_ref, rh_ref, rl_ref, *rest, alpha, n_blocks):
    x1_ref, lg_ref = rest[-2:]
    i = pl.program_id(0)

    @pl.when(i < n_blocks)
    def _rows():
        mix = jnp.dot(m_ref[...], w_ref[...], preferred_element_type=F32)
        x1 = _layer_norm(alpha * x_ref[...] + mix, g_ref[...], b_ref[...])
        x1_ref[...] = x1
        hi = x1.astype(BF16)
        lo = (x1 - hi.astype(F32)).astype(BF16)
        lg_ref[...] = (jnp.dot(hi, rh_ref[...], preferred_element_type=F32)
                       + jnp.dot(lo, rh_ref[...], preferred_element_type=F32)
                       + jnp.dot(hi, rl_ref[...], preferred_element_type=F32))

    if len(rest) == 3:
        tail_ref = rest[0]

        @pl.when(i == n_blocks)
        def _tail():
            n_tail = tail_ref.shape[0]
            x1_ref[:n_tail, :] = tail_ref[...]
            if n_tail < x1_ref.shape[0]:
                x1_ref[n_tail:, :] = jnp.zeros((x1_ref.shape[0] - n_tail, x1_ref.shape[1]), x1_ref.dtype)


def _mix_ln_router(merged_bf, w_out_bf, x, ln_g, ln_b, r_hi, r_lo, alpha, tail=None):
    n, d = x.shape
    tm = _row_tile(n, 256)
    nb = n // tm
    row = lambda i: (jnp.minimum(i, nb - 1), 0)
    fixed = lambda i: (0, 0)
    in_specs = [pl.BlockSpec((tm, d), row), pl.BlockSpec((d, d), fixed), pl.BlockSpec((tm, d), row),
                pl.BlockSpec((1, d), fixed), pl.BlockSpec((1, d), fixed),
                pl.BlockSpec((d, ROUTER_PAD), fixed), pl.BlockSpec((d, ROUTER_PAD), fixed)]
    args = [merged_bf, w_out_bf, x, ln_g, ln_b, r_hi, r_lo]
    extra = 0
    if tail is not None:
        assert tail.shape[0] <= tm and tail.shape[0] % SUBLANES == 0
        in_specs.append(pl.BlockSpec(tail.shape, fixed))
        args.append(tail)
        extra = 1
    return pl.pallas_call(
        functools.partial(_mix_ln_body, alpha=alpha, n_blocks=nb),
        grid=(nb + extra,),
        in_specs=in_specs,
        out_specs=(pl.BlockSpec((tm, d), lambda i: (i, 0)), pl.BlockSpec((tm, ROUTER_PAD), row)),
        out_shape=(jax.ShapeDtypeStruct((n + extra * tm, d), F32), jax.ShapeDtypeStruct((n, ROUTER_PAD), F32)),
        compiler_params=_cparams(("arbitrary",)),
        name="mix_ln_router",
    )(*args)


def _row_gather(idx_ref, base, count, src_hbm, dst_buf, sem):
    def start(r, carry):
        pltpu.make_async_copy(src_hbm.at[pl.ds(idx_ref[base + r], 1)], dst_buf.at[pl.ds(r, 1)], sem).start()
        return carry

    lax.fori_loop(0, count, start, 0, unroll=8)


def _row_gather_wait(count, src_hbm, dst_buf, sem):
    pltpu.make_async_copy(src_hbm.at[pl.ds(0, count)], dst_buf, sem).wait()


def _moe_body(te_ref, nt_ref, rt_ref, x_hbm, rw_ref, wg_ref, wu_ref, wd_ref, o_ref, xbuf, sems, *, tm):
    del te_ref
    t = pl.program_id(0)
    n_used = nt_ref[0]

    def fetch(tile, slot):
        _row_gather(rt_ref, tile * tm, tm, x_hbm, xbuf.at[slot], sems.at[slot])

    @pl.when(t == 0)
    def _prime():
        fetch(0, 0)

    @pl.when(t + 1 < n_used)
    def _prefetch():
        fetch(t + 1, (t + 1) % 2)

    @pl.when(t < n_used)
    def _run():
        slot = t % 2
        _row_gather_wait(tm, x_hbm, xbuf.at[slot], sems.at[slot])
        x = xbuf[slot].astype(BF16)
        gate = jnp.dot(x, wg_ref[...].astype(BF16), preferred_element_type=F32)
        up = jnp.dot(x, wu_ref[...].astype(BF16), preferred_element_type=F32)
        h = (gate * jax.nn.sigmoid(gate)) * up
        y = jnp.dot(h.astype(BF16), wd_ref[...].astype(BF16), preferred_element_type=F32)
        o_ref[...] = y * rw_ref[...]

    @pl.when(t >= n_used)
    def _skip():
        o_ref[...] = jnp.zeros(o_ref.shape, o_ref.dtype)


def _moe_experts(tile_expert, n_tiles_used, row_token, x_all, row_w, w_gate, w_up, w_down, tm):
    r = row_token.shape[0]
    d = x_all.shape[1]
    de = w_gate.shape[2]
    n_tiles = r // tm
    grid_spec = pltpu.PrefetchScalarGridSpec(
        num_scalar_prefetch=3,
        grid=(n_tiles,),
        in_specs=[pl.BlockSpec(memory_space=pl.ANY),
                  pl.BlockSpec((tm, 1), lambda t, te, nt, rt: (t, 0)),
                  pl.BlockSpec((None, d, de), lambda t, te, nt, rt: (te[t], 0, 0)),
                  pl.BlockSpec((None, d, de), lambda t, te, nt, rt: (te[t], 0, 0)),
                  pl.BlockSpec((None, de, d), lambda t, te, nt, rt: (te[t], 0, 0))],
        out_specs=pl.BlockSpec((tm, d), lambda t, te, nt, rt: (t, 0)),
        scratch_shapes=[pltpu.VMEM((2, tm, d), F32), pltpu.SemaphoreType.DMA((2,))],
    )
    return pl.pallas_call(
        functools.partial(_moe_body, tm=tm),
        grid_spec=grid_spec,
        out_shape=jax.ShapeDtypeStruct((r, d), F32),
        compiler_params=_cparams(("arbitrary",)),
        name="moe_experts",
    )(tile_expert, n_tiles_used, row_token, x_all, row_w, w_gate, w_up, w_down)


def _route(logits, b_rg, b_re):
    g_logits = logits[:, :N_GROUPS] + b_rg
    g_prob = jax.nn.softmax(g_logits, -1)
    g_sel = jnp.argmax(g_logits, -1)
    g_w = jnp.take_along_axis(g_prob, g_sel[:, None], -1)
    e_logits = (logits[:, N_GROUPS:N_GROUPS + N_EXPERTS] + b_re).reshape(-1, N_GROUPS, EXPERTS_PER_GROUP)
    e_logits = jnp.take_along_axis(e_logits, g_sel[:, None, None], axis=1)[:, 0]
    top_v, top_i = lax.top_k(e_logits, TOP_K)
    e_w = jax.nn.softmax(top_v, -1) * g_w
    e_id = g_sel[:, None] * EXPERTS_PER_GROUP + top_i
    return e_id.astype(jnp.int32), e_w


def _dispatch_plan(e_id, e_w, tm):
    n = e_id.shape[0]
    flat_e = e_id.reshape(-1)
    n_assign = flat_e.shape[0]
    onehot = (flat_e[:, None] == jnp.arange(N_EXPERTS, dtype=jnp.int32)[None, :]).astype(jnp.int32)
    ranks = jnp.cumsum(onehot, axis=0) - onehot
    rank = jnp.sum(ranks * onehot, axis=1)
    counts = jnp.sum(onehot, axis=0)
    padded = ((counts + tm - 1) // tm) * tm
    ends = jnp.cumsum(padded)
    starts = ends - padded
    pos = starts[flat_e] + rank
    n_tiles = (n_assign + tm - 1) // tm + N_EXPERTS
    rows = n_tiles * tm
    row_assign = jnp.zeros((rows,), jnp.int32).at[pos].set(jnp.arange(1, n_assign + 1, dtype=jnp.int32))
    live = row_assign > 0
    row_assign = jnp.maximum(row_assign - 1, 0)
    row_token = row_assign // TOP_K
    row_w = jnp.where(live, e_w.reshape(-1)[row_assign], 0.0)
    tile_start = jnp.arange(n_tiles, dtype=jnp.int32) * tm
    tile_expert = jnp.sum((tile_start[:, None] >= ends[None, :]).astype(jnp.int32), axis=1)
    n_used = (ends[-1] // tm).astype(jnp.int32)
    last_expert = jnp.sum((ends[-1] - 1 >= ends).astype(jnp.int32))
    tile_expert = jnp.minimum(tile_expert, last_expert).astype(jnp.int32)
    return row_token, row_w.reshape(rows, 1), tile_expert, n_used.reshape(1), pos.reshape(n, TOP_K)


def _ln2_body(pos_ref, x_ref, y_hbm, g_ref, b_ref, op_ref, os_ref, ybuf, sems, *, tm, alpha, prompt_blocks):
    i = pl.program_id(0)
    rows = TOP_K * tm

    def fetch(blk, slot):
        _row_gather(pos_ref, blk * rows, rows, y_hbm, ybuf.at[slot], sems.at[slot])

    @pl.when(i == 0)
    def _prime():
        fetch(0, 0)

    @pl.when(i + 1 < pl.num_programs(0))
    def _prefetch():
        fetch(i + 1, (i + 1) % 2)

    slot = i % 2
    _row_gather_wait(rows, y_hbm, ybuf.at[slot], sems.at[slot])
    moe = functools.reduce(jnp.add, [ybuf[slot, k * tm:(k + 1) * tm] for k in range(TOP_K)])
    out = _layer_norm(alpha * x_ref[...] + moe, g_ref[...], b_ref[...])

    @pl.when(i < prompt_blocks)
    def _prompt():
        op_ref[...] = out

    @pl.when(i >= prompt_blocks)
    def _sample():
        os_ref[...] = out


def _combine_ln(pos_blocked, x1_all, y_sorted, ln_g, ln_b, alpha, n_prompt, n, tm):
    d = x1_all.shape[1]
    assert n % tm == 0 and n_prompt % tm == 0 and n_prompt < n <= x1_all.shape[0]
    pb = n_prompt // tm
    grid_spec = pltpu.PrefetchScalarGridSpec(
        num_scalar_prefetch=1,
        grid=(n // tm,),
        in_specs=[pl.BlockSpec((tm, d), lambda i, pos: (i, 0)),
                  pl.BlockSpec(memory_space=pl.ANY),
                  pl.BlockSpec((1, d), lambda i, pos: (0, 0)), pl.BlockSpec((1, d), lambda i, pos: (0, 0))],
        out_specs=(pl.BlockSpec((tm, d), lambda i, pos: (jnp.minimum(i, pb - 1), 0)),
                   pl.BlockSpec((tm, d), lambda i, pos: (jnp.maximum(i - pb, 0), 0))),
        scratch_shapes=[pltpu.VMEM((2, TOP_K * tm, d), F32), pltpu.SemaphoreType.DMA((2,))],
    )
    return pl.pallas_call(
        functools.partial(_ln2_body, tm=tm, alpha=alpha, prompt_blocks=pb),
        grid_spec=grid_spec,
        out_shape=(jax.ShapeDtypeStruct((n_prompt, d), F32), jax.ShapeDtypeStruct((n - n_prompt, d), F32)),
        compiler_params=_cparams(("arbitrary",)),
        name="combine_ln",
    )(pos_blocked, x1_all, y_sorted, ln_g, ln_b)


def kernel(x_prompt, x_sample, cache_k, cache_v, state_pool, page_table, w_in, lambda_q1, lambda_k1, lambda_q2, lambda_k2, subln_g, w_pool_lin, pool_scale, w_pool_o, w_attn_o, w_out, ln1_g, ln1_b, w_router_group, b_router_group, w_router_expert, b_router_expert, w_exp_gate, w_exp_up, w_exp_down, ln2_g, ln2_b):
    depth = w_in.shape[0]
    assert depth == 1, "one layer per step"
    layer = 0
    batch, seq, d_model = x_prompt.shape
    dec_batch, dec_seq, _ = x_sample.shape
    n_pages = page_table.shape[1]
    page_size = cache_k.shape[2]
    d_pool = w_pool_lin.shape[1] * w_pool_lin.shape[2]
    past_len = n_pages * page_size
    alpha = (2.0 * depth) ** 0.25
    lam_init = 0.8 - 0.6 * math.exp(-0.3 * layer)
    n_p, n_s = batch * seq, dec_batch * dec_seq

    lam = (jnp.exp(jnp.sum(lambda_q1[layer].astype(F32) * lambda_k1[layer].astype(F32)))
           - jnp.exp(jnp.sum(lambda_q2[layer].astype(F32) * lambda_k2[layer].astype(F32))) + lam_init).reshape(1)
    gain = subln_g[layer].astype(F32).reshape(1, V_DIM)

    w_in_bf = w_in[layer].astype(BF16)
    w_lin_bf = w_pool_lin[layer].astype(BF16)
    scale = pool_scale[layer].astype(F32).reshape(1, d_pool)
    w_pool_o_bf = w_pool_o[layer].astype(BF16)
    w_attn_o_bf = w_attn_o[layer].astype(BF16)
    w_out_bf = w_out[layer].astype(BF16)
    ln1g, ln1b = ln1_g[layer].reshape(1, d_model), ln1_b[layer].reshape(1, d_model)
    ln2g, ln2b = ln2_g[layer].reshape(1, d_model), ln2_b[layer].reshape(1, d_model)
    w_router = jnp.concatenate([w_router_group[layer], w_router_expert[layer]], axis=1).astype(F32)
    w_router = jnp.pad(w_router, ((0, 0), (0, ROUTER_PAD - w_router.shape[1])))
    r_hi = w_router.astype(BF16)
    r_lo = (w_router - r_hi.astype(F32)).astype(BF16)

    xp = x_prompt.reshape(n_p, d_model)
    q_p, k_p, kb_p, v_p, vb_p, u_p, g_p = _in_projection(xp, w_in_bf, d_pool)
    attn_p = _prompt_attention(q_p, kb_p, vb_p, lam, gain.reshape(V_DIM, 1), batch, seq, lam_init)
    pooled_p = _pool_mixer(jnp.zeros((batch, HIST_ROWS, d_pool), F32), u_p, w_lin_bf, scale, batch, seq, 0)
    merged_p = _gated_merge(pooled_p, attn_p, g_p, w_pool_o_bf, w_attn_o_bf)

    xs = x_sample.reshape(n_s, d_model)
    q_s, k_s, kb_s, v_s, vb_s, u_s, g_s = _in_projection(xs, w_in_bf, d_pool)
    q5 = q_s.reshape(dec_batch, dec_seq, N_HEADS, 2, QK_DIM).transpose(0, 3, 2, 1, 4)
    zq = jnp.zeros_like(q5[:, 0])
    q_rows = jnp.stack([jnp.concatenate([q5[:, 0], zq], -1), jnp.concatenate([zq, q5[:, 1]], -1)], axis=1)
    q_rows = q_rows.reshape(dec_batch, 2 * N_HEADS * dec_seq, HEAD_W)
    new_rows = LANES
    assert dec_seq * N_HEADS <= new_rows
    pad_new = lambda a: jnp.pad(a.reshape(dec_batch, dec_seq * N_HEADS, HEAD_W),
                                ((0, 0), (0, new_rows - dec_seq * N_HEADS), (0, 0)))
    ck = cache_k[layer].reshape(cache_k.shape[1], page_size * N_HEADS, HEAD_W)
    cv = cache_v[layer].reshape(cache_v.shape[1], page_size * N_HEADS, V_DIM)
    attn_s = _sample_attention(q_rows, pad_new(kb_s), pad_new(vb_s), ck, cv, page_table.astype(jnp.int32),
                               lam, gain, dec_seq, lam_init)
    attn_s = attn_s.reshape(dec_batch, N_HEADS, dec_seq, V_DIM).transpose(0, 2, 1, 3).reshape(n_s, N_HEADS * V_DIM)
    seq_pad = -(-dec_seq // SUBLANES) * SUBLANES
    hist_s = jnp.pad(state_pool[layer].astype(F32), ((0, 0), (HIST_ROWS - POOL_HIST, 0), (0, 0)))
    u_s3 = u_s.reshape(dec_batch, dec_seq, d_pool)
    u_pad = jnp.pad(u_s3, ((0, 0), (0, seq_pad - dec_seq), (0, 0))).reshape(dec_batch * seq_pad, d_pool)
    pooled_s = _pool_mixer(hist_s, u_pad, w_lin_bf, scale, dec_batch, seq_pad, past_len)
    pooled_s = pooled_s.reshape(dec_batch, seq_pad, d_pool)[:, :dec_seq].reshape(n_s, d_pool)
    merged_s = _gated_merge(pooled_s, attn_s.astype(BF16), g_s, w_pool_o_bf, w_attn_o_bf)
    x1_s, lg_s = _mix_ln_router(merged_s, w_out_bf, xs, ln1g, ln1b, r_hi, r_lo, alpha)
    x1_all, lg_p = _mix_ln_router(merged_p, w_out_bf, xp, ln1g, ln1b, r_hi, r_lo, alpha, tail=x1_s)

    tm_moe = 256
    logits = jnp.concatenate([lg_p, lg_s], axis=0)
    e_id, e_w = _route(logits, b_router_group[layer].astype(F32), b_router_expert[layer].astype(F32))
    row_token, row_w, tile_expert, n_used, pos = _dispatch_plan(e_id, e_w, tm_moe)
    y_sorted = _moe_experts(tile_expert, n_used, row_token, x1_all, row_w, w_exp_gate[layer], w_exp_up[layer],
                            w_exp_down[layer], tm_moe)
    tm_ln = _row_tile(math.gcd(n_p, n_s), 128)
    pos_blocked = pos.reshape(-1, tm_ln, TOP_K).transpose(0, 2, 1).reshape(-1)
    y_p, y_s = _combine_ln(pos_blocked, x1_all, y_sorted, ln2g, ln2b, alpha, n_p, n_p + n_s, tm_ln)

    assert seq >= POOL_HIST
    new_pool_p = u_p.reshape(batch, seq, d_pool)[:, seq - POOL_HIST:]
    new_pool_s = jnp.concatenate([state_pool[layer].astype(F32), u_s3], axis=1)[:, -POOL_HIST:]
    return (y_p.reshape(batch, seq, d_model), y_s.reshape(dec_batch, dec_seq, d_model),
            k_p.reshape(1, batch, seq, N_HEADS, HEAD_W), v_p.reshape(1, batch, seq, N_HEADS, V_DIM),
            new_pool_p[None],
            k_s.reshape(1, dec_batch, dec_seq, N_HEADS, HEAD_W), v_s.reshape(1, dec_batch, dec_seq, N_HEADS, V_DIM),
            new_pool_s[None])
```

```python
import functools
import math

import jax
import jax.numpy as jnp
from jax import lax
from jax.experimental import pallas as pl
from jax.experimental.pallas import tpu as pltpu

F32 = jnp.float32
BF16 = jnp.bfloat16

N_HEADS = 8
QK_DIM = 128
V_DIM = 2 * QK_DIM
HEAD_W = 2 * QK_DIM
ATTN_SCALE = QK_DIM ** -0.5
LOG2_E = math.log2(math.e)
POOL_WINDOWS = (2, 4, 8, 16)
N_POOL_GROUPS = len(POOL_WINDOWS)
POOL_HIST = max(POOL_WINDOWS) - 1
HIST_ROWS = POOL_HIST + 1
N_GROUPS = 4
EXPERTS_PER_GROUP = 8
N_EXPERTS = N_GROUPS * EXPERTS_PER_GROUP
TOP_K = 2
LN_EPS = 1e-5

LANES = 128
SUBLANES = 8
VMEM_LIMIT_BYTES = 56 * 1024 * 1024
ROUTER_PAD = LANES
DECODE_PAGES_PER_STEP = (4, 2, 1)
DECODE_RING = 3

NEG_INF = float("-inf")


def _cparams(sem):
    return pltpu.CompilerParams(dimension_semantics=sem, vmem_limit_bytes=VMEM_LIMIT_BYTES)


def _row_tile(n, pref):
    t = min(pref, n)
    while n % t:
        t //= 2
    assert t >= SUBLANES, (n, pref)
    return t


def _proj_body(x_ref, w_ref, q_ref, k_ref, kb_ref, v_ref, vb_ref, u_ref, g_ref, xb_ref, *, bounds):
    j = pl.program_id(1)

    @pl.when(j == 0)
    def _cast():
        xb_ref[...] = x_ref[...].astype(BF16)

    acc = jnp.dot(xb_ref[...], w_ref[...], preferred_element_type=F32)
    q_hi, k_hi, v_hi, u_hi = bounds

    @pl.when(j < q_hi)
    def _q():
        q_ref[...] = (acc * (ATTN_SCALE * LOG2_E)).astype(BF16)

    @pl.when((j >= q_hi) & (j < k_hi))
    def _k():
        k_ref[...] = acc
        kb_ref[...] = acc.astype(BF16)

    @pl.when((j >= k_hi) & (j < v_hi))
    def _v():
        v_ref[...] = acc
        vb_ref[...] = acc.astype(BF16)

    @pl.when((j >= v_hi) & (j < u_hi))
    def _u():
        u_ref[...] = acc

    @pl.when(j >= u_hi)
    def _g():
        g_ref[...] = acc


def _in_projection(x, w_bf, d_pool):
    n, d = x.shape
    qw = N_HEADS * HEAD_W
    gw = w_bf.shape[1] - 3 * qw - d_pool
    tn = 512
    tm = _row_tile(n, 1024)
    bounds = (qw // tn, 2 * qw // tn, 3 * qw // tn, (3 * qw + d_pool) // tn)
    q_hi, k_hi, v_hi, u_hi = bounds
    n_col = w_bf.shape[1] // tn

    def seg(lo, hi):
        return lambda i, j: (i, jnp.clip(j - lo, 0, hi - lo - 1))

    out_shape = (
        jax.ShapeDtypeStruct((n, qw), BF16),
        jax.ShapeDtypeStruct((n, qw), F32), jax.ShapeDtypeStruct((n, qw), BF16),
        jax.ShapeDtypeStruct((n, qw), F32), jax.ShapeDtypeStruct((n, qw), BF16),
        jax.ShapeDtypeStruct((n, d_pool), F32),
        jax.ShapeDtypeStruct((n, gw), F32),
    )
    blk = lambda lo, hi: pl.BlockSpec((tm, tn), seg(lo, hi))
    return pl.pallas_call(
        functools.partial(_proj_body, bounds=bounds),
        grid=(n // tm, n_col),
        in_specs=[pl.BlockSpec((tm, d), lambda i, j: (i, 0)),
                  pl.BlockSpec((d, tn), lambda i, j: (0, j))],
        out_specs=(blk(0, q_hi), blk(q_hi, k_hi), blk(q_hi, k_hi), blk(k_hi, v_hi), blk(k_hi, v_hi),
                   blk(v_hi, u_hi), blk(u_hi, n_col)),
        out_shape=out_shape,
        scratch_shapes=[pltpu.VMEM((tm, d), BF16)],
        compiler_params=_cparams(("arbitrary", "arbitrary")),
        name="in_projection",
    )(x, w_bf)


def _diff_finalize(o1, o2, lam, gain, lam_init):
    o = o1 - lam * o2
    o = o * lax.rsqrt(jnp.mean(o * o, axis=-1, keepdims=True) + LN_EPS)
    return o * gain * (1.0 - lam_init)


def _flash_body(lam_ref, q_ref, k_ref, v_ref, g_ref, o_ref, qbd_ref, s0_ref, s1_ref, acc_ref, m_ref, l_ref,
                *, tq, lam_init):
    i = pl.program_id(2)
    q = q_ref[...]
    zeros = jnp.zeros((tq, QK_DIM), q.dtype)
    qbd_ref[:tq, :] = jnp.concatenate([q[:, :QK_DIM], zeros], axis=1)
    qbd_ref[tq:, :] = jnp.concatenate([zeros, q[:, QK_DIM:]], axis=1)
    m_ref[...] = jnp.full(m_ref.shape, NEG_INF, F32)
    l_ref[...] = jnp.zeros(l_ref.shape, F32)
    acc_ref[...] = jnp.zeros(acc_ref.shape, F32)

    def scores(j, dst_ref):
        ks = k_ref[pl.ds(pl.multiple_of(j * tq, tq), tq), :]
        dst_ref[...] = lax.dot_general(ks, qbd_ref[...], (((1,), (1,)), ((), ())), preferred_element_type=F32)

    def block(j, src_ref, masked):
        vs = v_ref[pl.ds(pl.multiple_of(j * tq, tq), tq), :]
        s = src_ref[...]
        if masked:
            key = lax.broadcasted_iota(jnp.int32, (tq, tq), 0)
            qry = lax.broadcasted_iota(jnp.int32, (tq, tq), 1)
            visible = key <= qry
            s = jnp.concatenate([jnp.where(visible, s[:, :tq], NEG_INF), jnp.where(visible, s[:, tq:], NEG_INF)],
                                axis=1)
        m_old = m_ref[...]
        m_new = jnp.maximum(m_old, jnp.max(s, axis=0, keepdims=True))
        p = jnp.exp2(s - m_new)
        a = jnp.exp2(m_old - m_new)
        l_ref[...] = a * l_ref[...] + jnp.sum(p, axis=0, keepdims=True)
        pv = lax.dot_general(vs, p.astype(BF16), (((0,), (0,)), ((), ())), preferred_element_type=F32)
        acc_ref[...] = a * acc_ref[...] + pv
        m_ref[...] = m_new

    def off_diag_pair(pair, c):
        j = 2 * pair
        scores(j + 1, s1_ref)
        block(j, s0_ref, False)
        scores(j + 2, s0_ref)
        block(j + 1, s1_ref, False)
        return c

    scores(0, s0_ref)
    lax.fori_loop(0, i // 2, off_diag_pair, 0)

    @pl.when(i % 2 == 0)
    def _diag_even():
        block(i, s0_ref, True)

    @pl.when(i % 2 == 1)
    def _diag_odd():
        scores(i, s1_ref)
        block(i - 1, s0_ref, False)
        block(i, s1_ref, True)

    o_t = acc_ref[...] / l_ref[...]
    o = o_t[:, :tq] - lam_ref[0] * o_t[:, tq:]
    o = o * lax.rsqrt(jnp.mean(o * o, axis=0, keepdims=True) + LN_EPS) * g_ref[...] * (1.0 - lam_init)
    o_ref[...] = o.T.astype(o_ref.dtype)


def _prompt_attention(q_bf, k_bf, v_bf, lam, gain_col, batch, seq, lam_init):
    tq = _row_tile(seq, 512)
    nq = seq // tq
    kv_spec = pl.BlockSpec((seq, HEAD_W), lambda b, h, i: (b, h))
    return pl.pallas_call(
        functools.partial(_flash_body, tq=tq, lam_init=lam_init),
        grid=(batch, N_HEADS, nq),
        in_specs=[pl.BlockSpec(memory_space=pltpu.SMEM),
                  pl.BlockSpec((tq, HEAD_W), lambda b, h, i: (b * nq + i, h)),
                  kv_spec, kv_spec,
                  pl.BlockSpec((V_DIM, 1), lambda b, h, i: (0, 0))],
        out_specs=pl.BlockSpec((tq, V_DIM), lambda b, h, i: (b * nq + i, h)),
        out_shape=jax.ShapeDtypeStruct((batch * seq, N_HEADS * V_DIM), BF16),
        scratch_shapes=[pltpu.VMEM((2 * tq, HEAD_W), BF16),
                        pltpu.VMEM((tq, 2 * tq), F32), pltpu.VMEM((tq, 2 * tq), F32),
                        pltpu.VMEM((V_DIM, 2 * tq), F32),
                        pltpu.VMEM((1, 2 * tq), F32), pltpu.VMEM((1, 2 * tq), F32)],
        compiler_params=_cparams(("arbitrary", "arbitrary", "arbitrary")),
        name="prompt_attention",
    )(lam, q_bf, k_bf, v_bf, gain_col)


def _decode_body(pt_ref, lam_ref, q_ref, kn_ref, vn_ref, kc_hbm, vc_hbm, bn_ref, bp_ref, g_ref, o_ref,
                 kbuf, vbuf, sems, acc_ref, m_ref, l_ref, *, n_chunk, n_groups, lam_init):
    p_idx = pl.program_id(1)
    g = pl.program_id(0) * pl.num_programs(1) + p_idx
    half = q_ref.shape[0] // 2
    q = q_ref[...]

    def page_copies(group, slot):
        copies = []
        for r in range(n_chunk):
            page = pt_ref[group * n_chunk + r]
            copies.append(pltpu.make_async_copy(kc_hbm.at[page], kbuf.at[slot, r], sems.at[slot]))
            copies.append(pltpu.make_async_copy(vc_hbm.at[page], vbuf.at[slot, r], sems.at[slot]))
        return copies

    def start_group(group):
        for cp in page_copies(group, group % DECODE_RING):
            cp.start()

    @pl.when(g == 0)
    def _prime():
        for ahead in range(min(DECODE_RING - 1, n_groups)):
            start_group(ahead)

    @pl.when(g + (DECODE_RING - 1) < n_groups)
    def _prefetch():
        start_group(g + (DECODE_RING - 1))

    slot = g % DECODE_RING
    for cp in page_copies(g, slot):
        cp.wait()
    kc_refs = [kbuf.at[slot, r] for r in range(n_chunk)]
    vc_refs = [vbuf.at[slot, r] for r in range(n_chunk)]

    def attend(blocks):
        scores = [lax.dot_general(q, k_bf, (((1,), (1,)), ((), ())), preferred_element_type=F32) + bias
                  for k_bf, _, bias in blocks]
        m_old = m_ref[...]
        m_new = jnp.maximum(m_old, jnp.max(functools.reduce(jnp.maximum, scores), axis=-1, keepdims=True))
        probs = [jnp.exp2(s - m_new) for s in scores]
        a = jnp.exp2(m_old - m_new)
        l_ref[...] = a * l_ref[...] + jnp.sum(functools.reduce(jnp.add, probs), axis=-1, keepdims=True)
        pv = [jnp.dot(p.astype(BF16), v_bf, preferred_element_type=F32) for p, (_, v_bf, _) in zip(probs, blocks)]
        acc_ref[...] = a * acc_ref[...] + functools.reduce(jnp.add, pv)
        m_ref[...] = m_new

    @pl.when(p_idx == 0)
    def _new_tokens():
        m_ref[...] = jnp.full(m_ref.shape, NEG_INF, F32)
        l_ref[...] = jnp.zeros(l_ref.shape, F32)
        acc_ref[...] = jnp.zeros(acc_ref.shape, F32)
        attend([(kn_ref[...], vn_ref[...], bn_ref[...])])

    attend([(kc[...].astype(BF16), vc[...].astype(BF16), bp_ref[...]) for kc, vc in zip(kc_refs, vc_refs)])

    @pl.when(p_idx == pl.num_programs(1) - 1)
    def _finish():
        o = acc_ref[...] / l_ref[...]
        o_ref[...] = _diff_finalize(o[:half], o[half:], lam_ref[0], g_ref[...], lam_init).astype(o_ref.dtype)


def _decode_bias(rows, cols, t_new, causal):
    r = jnp.arange(rows, dtype=jnp.int32)[:, None]
    c = jnp.arange(cols, dtype=jnp.int32)[None, :]
    ok = (c % N_HEADS) == ((r // t_new) % N_HEADS)
    if causal:
        ok = ok & ((c // N_HEADS) <= (r % t_new))
    return jnp.where(ok, 0.0, NEG_INF).astype(F32)


def _sample_attention(q_rows, k_new, v_new, cache_k, cache_v, page_table, lam, gain, t_new, lam_init):
    nb, rows, _ = q_rows.shape
    n_pages = page_table.shape[1]
    page_rows = cache_k.shape[1]
    new_rows = k_new.shape[1]
    bias_new = _decode_bias(rows, new_rows, t_new, True)
    bias_page = _decode_bias(rows, page_rows, t_new, False)
    n_chunk = max(c for c in DECODE_PAGES_PER_STEP if n_pages % c == 0)

    per_b = lambda r: pl.BlockSpec((None, r, HEAD_W), lambda b, p, pt: (b, 0, 0))
    fixed = lambda shape: pl.BlockSpec(shape, lambda b, p, pt: (0, 0))
    ring = (DECODE_RING, n_chunk, page_rows, HEAD_W)
    grid_spec = pltpu.PrefetchScalarGridSpec(
        num_scalar_prefetch=1,
        grid=(nb, n_pages // n_chunk),
        in_specs=[pl.BlockSpec(memory_space=pltpu.SMEM),
                  per_b(rows), per_b(new_rows), per_b(new_rows),
                  pl.BlockSpec(memory_space=pl.ANY), pl.BlockSpec(memory_space=pl.ANY),
                  fixed((rows, new_rows)), fixed((rows, page_rows)), fixed((1, V_DIM))],
        out_specs=pl.BlockSpec((None, rows // 2, V_DIM), lambda b, p, pt: (b, 0, 0)),
        scratch_shapes=[pltpu.VMEM(ring, F32), pltpu.VMEM(ring, F32), pltpu.SemaphoreType.DMA((DECODE_RING,)),
                        pltpu.VMEM((rows, V_DIM), F32), pltpu.VMEM((rows, 1), F32), pltpu.VMEM((rows, 1), F32)],
    )
    return pl.pallas_call(
        functools.partial(_decode_body, n_chunk=n_chunk, n_groups=nb * (n_pages // n_chunk), lam_init=lam_init),
        grid_spec=grid_spec,
        out_shape=jax.ShapeDtypeStruct((nb, rows // 2, V_DIM), F32),
        compiler_params=_cparams(("arbitrary", "arbitrary")),
        name="sample_attention",
    )(page_table.reshape(-1), lam, q_rows, k_new, v_new, cache_k, cache_v, bias_new, bias_page, gain)


def _pool_body(hist_ref, prev_ref, u_ref, w_ref, sc_ref, o_ref, *, tm, pos0):
    i = pl.program_id(1)
    u = u_ref[...]
    head = jnp.where(i == 0, hist_ref[...], prev_ref[...])
    ext = jnp.concatenate([head, u], axis=0)
    pos = pos0 + i * tm + lax.broadcasted_iota(jnp.int32, (tm, 1), 0)
    gdim = w_ref.shape[1]
    for g, win in enumerate(POOL_WINDOWS):
        sl = slice(g * gdim, (g + 1) * gdim)
        s = ext[:, sl]
        span = 1
        while span < win:
            s = s + pltpu.roll(s, span, axis=0)
            span *= 2
        cnt = jnp.minimum(win, pos + 1).astype(F32)
        pooled = s[HIST_ROWS:] / cnt - u[:, sl]
        mixed = jnp.dot(pooled.astype(BF16), w_ref[g], preferred_element_type=F32)
        o_ref[:, sl] = (mixed * sc_ref[:, sl]).astype(o_ref.dtype)


def _pool_mixer(hist, u, w_lin_bf, scale, batch, seq, pos0):
    d_pool = u.shape[1]
    tm = _row_tile(seq, 512)
    nt = seq // tm
    per = tm // HIST_ROWS
    return pl.pallas_call(
        functools.partial(_pool_body, tm=tm, pos0=pos0),
        grid=(batch, nt),
        in_specs=[pl.BlockSpec((None, HIST_ROWS, d_pool), lambda b, i: (b, 0, 0)),
                  pl.BlockSpec((HIST_ROWS, d_pool), lambda b, i: (jnp.maximum((b * nt + i) * per - 1, 0), 0)),
                  pl.BlockSpec((tm, d_pool), lambda b, i: (b * nt + i, 0)),
                  pl.BlockSpec(w_lin_bf.shape, lambda b, i: (0, 0, 0)),
                  pl.BlockSpec((1, d_pool), lambda b, i: (0, 0))],
        out_specs=pl.BlockSpec((tm, d_pool), lambda b, i: (b * nt + i, 0)),
        out_shape=jax.ShapeDtypeStruct(u.shape, BF16),
        compiler_params=_cparams(("arbitrary", "arbitrary")),
        name="pool_mixer",
    )(hist, u, u, w_lin_bf, scale)


def _merge_body(pool_ref, attn_ref, ga_ref, gb_ref, wp_ref, wa_ref, o_ref):
    pool_branch = jnp.dot(pool_ref[...], wp_ref[...], preferred_element_type=F32)
    attn_branch = jnp.dot(attn_ref[...], wa_ref[...], preferred_element_type=F32)
    merged = jax.nn.sigmoid(ga_ref[...]) * pool_branch + jax.nn.sigmoid(gb_ref[...]) * attn_branch
    o_ref[...] = merged.astype(o_ref.dtype)


def _gated_merge(pooled_bf, attn_bf, gates, w_pool_o_bf, w_attn_o_bf):
    n, d_pool = pooled_bf.shape
    d = w_pool_o_bf.shape[1]
    tm = _row_tile(n, 1024)
    tn = 512
    nj = d // tn
    return pl.pallas_call(
        _merge_body,
        grid=(n // tm, nj),
        in_specs=[pl.BlockSpec((tm, d_pool), lambda i, j: (i, 0)),
                  pl.BlockSpec((tm, attn_bf.shape[1]), lambda i, j: (i, 0)),
                  pl.BlockSpec((tm, tn), lambda i, j: (i, j)),
                  pl.BlockSpec((tm, tn), lambda i, j: (i, nj + j)),
                  pl.BlockSpec((d_pool, tn), lambda i, j: (0, j)),
                  pl.BlockSpec((attn_bf.shape[1], tn), lambda i, j: (0, j))],
        out_specs=pl.BlockSpec((tm, tn), lambda i, j: (i, j)),
        out_shape=jax.ShapeDtypeStruct((n, d), BF16),
        compiler_params=_cparams(("arbitrary", "arbitrary")),
        name="gated_merge",
    )(pooled_bf, attn_bf, gates, gates, w_pool_o_bf, w_attn_o_bf)


def _layer_norm(y, g, b):
    mu = jnp.mean(y, axis=-1, keepdims=True)
    yc = y - mu
    var = jnp.mean(yc * yc, axis=-1, keepdims=True)
    return yc * lax.rsqrt(var + LN_EPS) * g + b


def _mix_ln_body(m_ref, w_ref, x_ref, g_ref, b_ref, rh_ref, rl_ref, *rest, alpha, n_blocks):
    x1_ref, lg_ref = rest[-2:]
    i = pl.program_id(0)

    @pl.when(i < n_blocks)
    def _rows():
        mix = jnp.dot(m_ref[...], w_ref[...], preferred_element_type=F32)
        x1 = _layer_norm(alpha * x_ref[...] + mix, g_ref[...], b_ref[...])
        x1_ref[...] = x1
        hi = x1.astype(BF16)
        lo = (x1 - hi.astype(F32)).astype(BF16)
        lg_ref[...] = (jnp.dot(hi, rh_ref[...], preferred_element_type=F32)
                       + jnp.dot(lo, rh_ref[...], preferred_element_type=F32)
                       + jnp.dot(hi, rl_ref[...], preferred_element_type=F32))

    if len(rest) == 3:
        tail_ref = rest[0]

        @pl.when(i == n_blocks)
        def _tail():
            n_tail = tail_ref.shape[0]
            x1_ref[:n_tail, :] = tail_ref[...]
            if n_tail < x1_ref.shape[0]:
                x1_ref[n_tail:, :] = jnp.zeros((x1_ref.shape[0] - n_tail, x1_ref.shape[1]), x1_ref.dtype)


def _mix_ln_router(merged_bf, w_out_bf, x, ln_g, ln_b, r_hi, r_lo, alpha, tail=None):
    n, d = x.shape
    tm = _row_tile(n, 256)
    nb = n // tm
    row = lambda i: (jnp.minimum(i, nb - 1), 0)
    fixed = lambda i: (0, 0)
    in_specs = [pl.BlockSpec((tm, d), row), pl.BlockSpec((d, d), fixed), pl.BlockSpec((tm, d), row),
                pl.BlockSpec((1, d), fixed), pl.BlockSpec((1, d), fixed),
                pl.BlockSpec((d, ROUTER_PAD), fixed), pl.BlockSpec((d, ROUTER_PAD), fixed)]
    args = [merged_bf, w_out_bf, x, ln_g, ln_b, r_hi, r_lo]
    extra = 0
    if tail is not None:
        assert tail.shape[0] <= tm and tail.shape[0] % SUBLANES == 0
        in_specs.append(pl.BlockSpec(tail.shape, fixed))
        args.append(tail)
        extra = 1
    return pl.pallas_call(
        functools.partial(_mix_ln_body, alpha=alpha, n_blocks=nb),
        grid=(nb + extra,),
        in_specs=in_specs,
        out_specs=(pl.BlockSpec((tm, d), lambda i: (i, 0)), pl.BlockSpec((tm, ROUTER_PAD), row)),
        out_shape=(jax.ShapeDtypeStruct((n + extra * tm, d), F32), jax.ShapeDtypeStruct((n, ROUTER_PAD), F32)),
        compiler_params=_cparams(("arbitrary",)),
        name="mix_ln_router",
    )(*args)


def _row_gather(idx_ref, base, count, src_hbm, dst_buf, sem, split_priorities=False):
    for r in range(count):
        copy = pltpu.make_async_copy(src_hbm.at[pl.ds(idx_ref[base + r], 1)], dst_buf.at[pl.ds(r, 1)], sem)
        copy.start(priority=r % 2 if split_priorities else 0)


def _row_gather_wait(count, src_hbm, dst_buf, sem):
    pltpu.make_async_copy(src_hbm.at[pl.ds(0, count)], dst_buf, sem).wait()


def _moe_body(te_ref, nt_ref, rt_ref, x_hbm, rw_ref, wg_ref, wu_ref, wd_ref, o_ref, xbuf, sems, *, tm):
    del te_ref
    t = pl.program_id(0)
    n_used = nt_ref[0]

    def fetch(tile, slot):
        _row_gather(rt_ref, tile * tm, tm, x_hbm, xbuf.at[slot], sems.at[slot])

    @pl.when(t == 0)
    def _prime():
        fetch(0, 0)

    @pl.when(t + 1 < n_used)
    def _prefetch():
        fetch(t + 1, (t + 1) % 2)

    @pl.when(t < n_used)
    def _run():
        slot = t % 2
        _row_gather_wait(tm, x_hbm, xbuf.at[slot], sems.at[slot])
        x = xbuf[slot].astype(BF16)
        gate = jnp.dot(x, wg_ref[...].astype(BF16), preferred_element_type=F32)
        up = jnp.dot(x, wu_ref[...].astype(BF16), preferred_element_type=F32)
        h = (gate * jax.nn.sigmoid(gate)) * up
        y = jnp.dot(h.astype(BF16), wd_ref[...].astype(BF16), preferred_element_type=F32)
        o_ref[...] = y * rw_ref[...]

    @pl.when(t >= n_used)
    def _skip():
        o_ref[...] = jnp.zeros(o_ref.shape, o_ref.dtype)


def _moe_experts(tile_expert, n_tiles_used, row_token, x_all, row_w, w_gate, w_up, w_down, tm):
    r = row_token.shape[0]
    d = x_all.shape[1]
    de = w_gate.shape[2]
    n_tiles = r // tm
    grid_spec = pltpu.PrefetchScalarGridSpec(
        num_scalar_prefetch=3,
        grid=(n_tiles,),
        in_specs=[pl.BlockSpec(memory_space=pl.ANY),
                  pl.BlockSpec((tm, 1), lambda t, te, nt, rt: (t, 0)),
                  pl.BlockSpec((None, d, de), lambda t, te, nt, rt: (te[t], 0, 0)),
                  pl.BlockSpec((None, d, de), lambda t, te, nt, rt: (te[t], 0, 0)),
                  pl.BlockSpec((None, de, d), lambda t, te, nt, rt: (te[t], 0, 0))],
        out_specs=pl.BlockSpec((tm, d), lambda t, te, nt, rt: (t, 0)),
        scratch_shapes=[pltpu.VMEM((2, tm, d), F32), pltpu.SemaphoreType.DMA((2,))],
    )
    return pl.pallas_call(
        functools.partial(_moe_body, tm=tm),
        grid_spec=grid_spec,
        out_shape=jax.ShapeDtypeStruct((r, d), F32),
        compiler_params=_cparams(("arbitrary",)),
        name="moe_experts",
    )(tile_expert, n_tiles_used, row_token, x_all, row_w, w_gate, w_up, w_down)


def _route(logits, b_rg, b_re):
    g_logits = logits[:, :N_GROUPS] + b_rg
    g_prob = jax.nn.softmax(g_logits, -1)
    g_sel = jnp.argmax(g_logits, -1)
    g_w = jnp.take_along_axis(g_prob, g_sel[:, None], -1)
    e_logits = (logits[:, N_GROUPS:N_GROUPS + N_EXPERTS] + b_re).reshape(-1, N_GROUPS, EXPERTS_PER_GROUP)
    e_logits = jnp.take_along_axis(e_logits, g_sel[:, None, None], axis=1)[:, 0]
    top_v, top_i = lax.top_k(e_logits, TOP_K)
    e_w = jax.nn.softmax(top_v, -1) * g_w
    e_id = g_sel[:, None] * EXPERTS_PER_GROUP + top_i
    return e_id.astype(jnp.int32), e_w


def _dispatch_plan(e_id, e_w, tm):
    n = e_id.shape[0]
    flat_e = e_id.reshape(-1)
    n_assign = flat_e.shape[0]
    onehot = (flat_e[:, None] == jnp.arange(N_EXPERTS, dtype=jnp.int32)[None, :]).astype(jnp.int32)
    ranks = jnp.cumsum(onehot, axis=0) - onehot
    rank = jnp.sum(ranks * onehot, axis=1)
    counts = jnp.sum(onehot, axis=0)
    padded = ((counts + tm - 1) // tm) * tm
    ends = jnp.cumsum(padded)
    starts = ends - padded
    pos = starts[flat_e] + rank
    n_tiles = (n_assign + tm - 1) // tm + N_EXPERTS
    rows = n_tiles * tm
    row_assign = jnp.zeros((rows,), jnp.int32).at[pos].set(jnp.arange(1, n_assign + 1, dtype=jnp.int32))
    live = row_assign > 0
    row_assign = jnp.maximum(row_assign - 1, 0)
    row_token = row_assign // TOP_K
    row_w = jnp.where(live, e_w.reshape(-1)[row_assign], 0.0)
    tile_start = jnp.arange(n_tiles, dtype=jnp.int32) * tm
    tile_expert = jnp.sum((tile_start[:, None] >= ends[None, :]).astype(jnp.int32), axis=1)
    n_used = (ends[-1] // tm).astype(jnp.int32)
    last_expert = jnp.sum((ends[-1] - 1 >= ends).astype(jnp.int32))
    tile_expert = jnp.minimum(tile_expert, last_expert).astype(jnp.int32)
    return row_token, row_w.reshape(rows, 1), tile_expert, n_used.reshape(1), pos.reshape(n, TOP_K)


def _ln2_body(pos_ref, x_ref, y_hbm, g_ref, b_ref, op_ref, os_ref, ybuf, sems, *, tm, alpha, prompt_blocks):
    i = pl.program_id(0)
    rows = TOP_K * tm

    def fetch(blk, slot):
        _row_gather(pos_ref, blk * rows, rows, y_hbm, ybuf.at[slot], sems.at[slot], split_priorities=True)

    @pl.when(i == 0)
    def _prime():
        fetch(0, 0)

    @pl.when(i + 1 < pl.num_programs(0))
    def _prefetch():
        fetch(i + 1, (i + 1) % 2)

    slot = i % 2
    _row_gather_wait(rows, y_hbm, ybuf.at[slot], sems.at[slot])
    moe = functools.reduce(jnp.add, [ybuf[slot, k * tm:(k + 1) * tm] for k in range(TOP_K)])
    out = _layer_norm(alpha * x_ref[...] + moe, g_ref[...], b_ref[...])

    @pl.when(i < prompt_blocks)
    def _prompt():
        op_ref[...] = out

    @pl.when(i >= prompt_blocks)
    def _sample():
        os_ref[...] = out


def _combine_ln(pos_blocked, x1_all, y_sorted, ln_g, ln_b, alpha, n_prompt, n, tm):
    d = x1_all.shape[1]
    assert n % tm == 0 and n_prompt % tm == 0 and n_prompt < n <= x1_all.shape[0]
    pb = n_prompt // tm
    grid_spec = pltpu.PrefetchScalarGridSpec(
        num_scalar_prefetch=1,
        grid=(n // tm,),
        in_specs=[pl.BlockSpec((tm, d), lambda i, pos: (i, 0)),
                  pl.BlockSpec(memory_space=pl.ANY),
                  pl.BlockSpec((1, d), lambda i, pos: (0, 0)), pl.BlockSpec((1, d), lambda i, pos: (0, 0))],
        out_specs=(pl.BlockSpec((tm, d), lambda i, pos: (jnp.minimum(i, pb - 1), 0)),
                   pl.BlockSpec((tm, d), lambda i, pos: (jnp.maximum(i - pb, 0), 0))),
        scratch_shapes=[pltpu.VMEM((2, TOP_K * tm, d), F32), pltpu.SemaphoreType.DMA((2,))],
    )
    return pl.pallas_call(
        functools.partial(_ln2_body, tm=tm, alpha=alpha, prompt_blocks=pb),
        grid_spec=grid_spec,
        out_shape=(jax.ShapeDtypeStruct((n_prompt, d), F32), jax.ShapeDtypeStruct((n - n_prompt, d), F32)),
        compiler_params=_cparams(("arbitrary",)),
        name="combine_ln",
    )(pos_blocked, x1_all, y_sorted, ln_g, ln_b)


def kernel(x_prompt, x_sample, cache_k, cache_v, state_pool, page_table, w_in, lambda_q1, lambda_k1, lambda_q2, lambda_k2, subln_g, w_pool_lin, pool_scale, w_pool_o, w_attn_o, w_out, ln1_g, ln1_b, w_router_group, b_router_group, w_router_expert, b_router_expert, w_exp_gate, w_exp_up, w_exp_down, ln2_g, ln2_b):
    depth = w_in.shape[0]
    assert depth == 1, "one layer per step"
    layer = 0
    batch, seq, d_model = x_prompt.shape
    dec_batch, dec_seq, _ = x_sample.shape
    n_pages = page_table.shape[1]
    page_size = cache_k.shape[2]
    d_pool = w_pool_lin.shape[1] * w_pool_lin.shape[2]
    past_len = n_pages * page_size
    alpha = (2.0 * depth) ** 0.25
    lam_init = 0.8 - 0.6 * math.exp(-0.3 * layer)
    n_p, n_s = batch * seq, dec_batch * dec_seq

    lam = (jnp.exp(jnp.sum(lambda_q1[layer].astype(F32) * lambda_k1[layer].astype(F32)))
           - jnp.exp(jnp.sum(lambda_q2[layer].astype(F32) * lambda_k2[layer].astype(F32))) + lam_init).reshape(1)
    gain = subln_g[layer].astype(F32).reshape(1, V_DIM)

    w_in_bf = w_in[layer].astype(BF16)
    w_lin_bf = w_pool_lin[layer].astype(BF16)
    scale = pool_scale[layer].astype(F32).reshape(1, d_pool)
    w_pool_o_bf = w_pool_o[layer].astype(BF16)
    w_attn_o_bf = w_attn_o[layer].astype(BF16)
    w_out_bf = w_out[layer].astype(BF16)
    ln1g, ln1b = ln1_g[layer].reshape(1, d_model), ln1_b[layer].reshape(1, d_model)
    ln2g, ln2b = ln2_g[layer].reshape(1, d_model), ln2_b[layer].reshape(1, d_model)
    w_router = jnp.concatenate([w_router_group[layer], w_router_expert[layer]], axis=1).astype(F32)
    w_router = jnp.pad(w_router, ((0, 0), (0, ROUTER_PAD - w_router.shape[1])))
    r_hi = w_router.astype(BF16)
    r_lo = (w_router - r_hi.astype(F32)).astype(BF16)

    xp = x_prompt.reshape(n_p, d_model)
    q_p, k_p, kb_p, v_p, vb_p, u_p, g_p = _in_projection(xp, w_in_bf, d_pool)
    attn_p = _prompt_attention(q_p, kb_p, vb_p, lam, gain.reshape(V_DIM, 1), batch, seq, lam_init)
    pooled_p = _pool_mixer(jnp.zeros((batch, HIST_ROWS, d_pool), F32), u_p, w_lin_bf, scale, batch, seq, 0)
    merged_p = _gated_merge(pooled_p, attn_p, g_p, w_pool_o_bf, w_attn_o_bf)

    xs = x_sample.reshape(n_s, d_model)
    q_s, k_s, kb_s, v_s, vb_s, u_s, g_s = _in_projection(xs, w_in_bf, d_pool)
    q5 = q_s.reshape(dec_batch, dec_seq, N_HEADS, 2, QK_DIM).transpose(0, 3, 2, 1, 4)
    zq = jnp.zeros_like(q5[:, 0])
    q_rows = jnp.stack([jnp.concatenate([q5[:, 0], zq], -1), jnp.concatenate([zq, q5[:, 1]], -1)], axis=1)
    q_rows = q_rows.reshape(dec_batch, 2 * N_HEADS * dec_seq, HEAD_W)
    new_rows = LANES
    assert dec_seq * N_HEADS <= new_rows
    pad_new = lambda a: jnp.pad(a.reshape(dec_batch, dec_seq * N_HEADS, HEAD_W),
                                ((0, 0), (0, new_rows - dec_seq * N_HEADS), (0, 0)))
    ck = cache_k[layer].reshape(cache_k.shape[1], page_size * N_HEADS, HEAD_W)
    cv = cache_v[layer].reshape(cache_v.shape[1], page_size * N_HEADS, V_DIM)
    attn_s = _sample_attention(q_rows, pad_new(kb_s), pad_new(vb_s), ck, cv, page_table.astype(jnp.int32),
                               lam, gain, dec_seq, lam_init)
    attn_s = attn_s.reshape(dec_batch, N_HEADS, dec_seq, V_DIM).transpose(0, 2, 1, 3).reshape(n_s, N_HEADS * V_DIM)
    seq_pad = -(-dec_seq // SUBLANES) * SUBLANES
    hist_s = jnp.pad(state_pool[layer].astype(F32), ((0, 0), (HIST_ROWS - POOL_HIST, 0), (0, 0)))
    u_s3 = u_s.reshape(dec_batch, dec_seq, d_pool)
    u_pad = jnp.pad(u_s3, ((0, 0), (0, seq_pad - dec_seq), (0, 0))).reshape(dec_batch * seq_pad, d_pool)
    pooled_s = _pool_mixer(hist_s, u_pad, w_lin_bf, scale, dec_batch, seq_pad, past_len)
    pooled_s = pooled_s.reshape(dec_batch, seq_pad, d_pool)[:, :dec_seq].reshape(n_s, d_pool)
    merged_s = _gated_merge(pooled_s, attn_s.astype(BF16), g_s, w_pool_o_bf, w_attn_o_bf)
    x1_s, lg_s = _mix_ln_router(merged_s, w_out_bf, xs, ln1g, ln1b, r_hi, r_lo, alpha)
    x1_all, lg_p = _mix_ln_router(merged_p, w_out_bf, xp, ln1g, ln1b, r_hi, r_lo, alpha, tail=x1_s)

    tm_moe = 256
    logits = jnp.concatenate([lg_p, lg_s], axis=0)
    e_id, e_w = _route(logits, b_router_group[layer].astype(F32), b_router_expert[layer].astype(F32))
    row_token, row_w, tile_expert, n_used, pos = _dispatch_plan(e_id, e_w, tm_moe)
    y_sorted = _moe_experts(tile_expert, n_used, row_token, x1_all, row_w, w_exp_gate[layer], w_exp_up[layer],
                            w_exp_down[layer], tm_moe)
    tm_ln = _row_tile(math.gcd(n_p, n_s), 128)
    pos_blocked = pos.reshape(-1, tm_ln, TOP_K).transpose(0, 2, 1).reshape(-1)
    y_p, y_s = _combine_ln(pos_blocked, x1_all, y_sorted, ln2g, ln2b, alpha, n_p, n_p + n_s, tm_ln)

    assert seq >= POOL_HIST
    new_pool_p = u_p.reshape(batch, seq, d_pool)[:, seq - POOL_HIST:]
    new_pool_s = jnp.concatenate([state_pool[layer].astype(F32), u_s3], axis=1)[:, -POOL_HIST:]
    return (y_p.reshape(batch, seq, d_model), y_s.reshape(dec_batch, dec_seq, d_model),
            k_p.reshape(1, batch, seq, N_HEADS, HEAD_W), v_p.reshape(1, batch, seq, N_HEADS, V_DIM),
            new_pool_p[None],
            k_s.reshape(1, dec_batch, dec_seq, N_HEADS, HEAD_W), v_s.reshape(1, dec_batch, dec_seq, N_HEADS, V_DIM),
            new_pool_s[None])
```
